```python
import jax, jax.numpy as jnp
from jax import lax
import numpy as np

D_MODEL = 4096
BATCH = 1
SEQ = 8192
DEPTH = 1
DEC_BATCH = 32
DEC_SEQ = 4
PAST_LEN = 8192
PAGE_SIZE = 128

MIX_W = D_MODEL
ATT_W = MIX_W // 2
CONV_W = MIX_W - ATT_W
HEAD_DIM = 128
N_HEADS = ATT_W // HEAD_DIM
MOBA_BLOCK = 256
MOBA_TOP_K = 3
CONV_WIDTH = 31
CONV_CTX = CONV_WIDTH - 1
Q_CHUNK = 32
RMS_EPS = 1e-6
LN_EPS = 1e-5
N_IN = 4 * ATT_W + 3 * CONV_W

kernel_name = "hymba_moba_conformer_decode_step"


def rms_norm(x, g):
    x32 = x.astype(jnp.float32)
    y = x32 * lax.rsqrt(jnp.mean(x32 * x32, axis=-1, keepdims=True) + RMS_EPS)
    return (y * g.astype(jnp.float32)).astype(x.dtype)


def layer_norm(x, g, b):
    x32 = x.astype(jnp.float32)
    mu = jnp.mean(x32, axis=-1, keepdims=True)
    xc = x32 - mu
    var = jnp.mean(xc * xc, axis=-1, keepdims=True)
    y = xc * lax.rsqrt(var + LN_EPS) * g.astype(jnp.float32) + b.astype(jnp.float32)
    return y.astype(x.dtype)


def split_proj(h, w_in):
    z = h @ w_in
    cuts = [ATT_W, 2 * ATT_W, 3 * ATT_W, 4 * ATT_W, 4 * ATT_W + CONV_W, 4 * ATT_W + 2 * CONV_W]
    q, k, v, ga, cv, cg, gc = jnp.split(z, cuts, axis=-1)
    B, T = h.shape[0], h.shape[1]
    hd = (B, T, N_HEADS, HEAD_DIM)
    return q.reshape(hd), k.reshape(hd), v.reshape(hd), ga, cv, cg, gc


def moba_attend_sequence(q, k, v, q_pos):
    L = k.shape[0]
    nb = -(-L // MOBA_BLOCK)
    pad = nb * MOBA_BLOCK - L
    kp = jnp.pad(k, ((0, pad), (0, 0), (0, 0)))
    vp = jnp.pad(v, ((0, pad), (0, 0), (0, 0)))
    kh = kp.reshape(nb, MOBA_BLOCK, N_HEADS, HEAD_DIM).transpose(2, 0, 1, 3)
    vh = vp.reshape(nb, MOBA_BLOCK, N_HEADS, HEAD_DIM).transpose(2, 0, 1, 3)
    k_mean = jnp.mean(kh.astype(jnp.float32), axis=2)
    n_sel = min(MOBA_TOP_K, nb - 1)
    head_ix = jnp.arange(N_HEADS)[None, :, None]
    scale = HEAD_DIM ** -0.5

    def chunk(args):
        qc, pc = args
        c = qc.shape[0]
        own = pc // MOBA_BLOCK
        own_idx = jnp.broadcast_to(own[:, None, None], (c, N_HEADS, 1)).astype(jnp.int32)
        if n_sel > 0:
            gate = jnp.einsum('chd,hnd->chn', qc.astype(jnp.float32), k_mean)
            fully_past = jnp.arange(nb)[None, None, :] < own[:, None, None]
            gate = jnp.where(fully_past, gate, -jnp.inf)
            _, top = lax.top_k(gate, n_sel)
            sel_ok = jnp.broadcast_to(jnp.arange(n_sel)[None, None, :] < own[:, None, None], top.shape)
            idx = jnp.concatenate([top.astype(jnp.int32), own_idx], axis=-1)
            block_ok = jnp.concatenate([sel_ok, jnp.ones(own_idx.shape, dtype=bool)], axis=-1)
        else:
            idx = own_idx
            block_ok = jnp.ones(own_idx.shape, dtype=bool)
        kg = kh[head_ix, idx]
        vg = vh[head_ix, idx]
        s = jnp.einsum('chd,chjkd->chjk', qc, kg).astype(jnp.float32) * scale
        key_pos = idx[..., None] * MOBA_BLOCK + jnp.arange(MOBA_BLOCK)
        mask = block_ok[..., None] & (key_pos <= pc[:, None, None, None])
        s = jnp.where(mask, s, -jnp.inf)
        j = s.shape[2]
        p = jax.nn.softmax(s.reshape(c, N_HEADS, j * MOBA_BLOCK), axis=-1).reshape(s.shape)
        return jnp.einsum('chjk,chjkd->chd', p.astype(vg.dtype), vg)

    nq = q.shape[0]
    cq = Q_CHUNK if nq % Q_CHUNK == 0 else nq
    out = lax.map(chunk, (q.reshape(nq // cq, cq, N_HEADS, HEAD_DIM), q_pos.reshape(nq // cq, cq)))
    return out.reshape(nq, N_HEADS, HEAD_DIM)


def conv_branch(cv, cg, ctx, w_dw, b_dw, ln_g, ln_b, w_pw2):
    glu = cv * jax.nn.sigmoid(cg)
    xpad = jnp.concatenate([ctx.astype(glu.dtype), glu], axis=1)
    y = lax.conv_general_dilated(xpad, w_dw[:, None, :].astype(glu.dtype), (1,), 'VALID',
                                 dimension_numbers=('NWC', 'WIO', 'NWC'),
                                 feature_group_count=CONV_W) + b_dw
    y = jax.nn.silu(layer_norm(y, ln_g, ln_b))
    return y @ w_pw2, xpad[:, -CONV_CTX:]


def mix_out(attn, ga, conv, gc, w_out):
    B, T = ga.shape[0], ga.shape[1]
    z = jnp.concatenate([attn.reshape(B, T, ATT_W) * jax.nn.silu(ga), conv * jax.nn.silu(gc)], axis=-1)
    return z @ w_out


def setup_inputs(seed: int = 0) -> dict:
    key = jax.random.key(seed)
    ks = jax.random.split(key, 16)
    n_pages = PAST_LEN // PAGE_SIZE
    n_used = DEC_BATCH * n_pages
    n_pool = (n_used * 5 + 3) // 4
    f32 = jnp.float32
    page_table = jax.random.permutation(ks[0], n_pool)[:n_used].reshape(DEC_BATCH, n_pages).astype(jnp.int32)
    return {
        "x_prompt": jax.random.normal(ks[1], (BATCH, SEQ, D_MODEL), f32),
        "x_sample": jax.random.normal(ks[2], (DEC_BATCH, DEC_SEQ, D_MODEL), f32),
        "cache_k": jax.random.normal(ks[3], (DEPTH, n_pool, PAGE_SIZE, N_HEADS, HEAD_DIM), f32),
        "cache_v": jax.random.normal(ks[4], (DEPTH, n_pool, PAGE_SIZE, N_HEADS, HEAD_DIM), f32),
        "state_conv": 0.5 * jax.random.normal(ks[5], (DEPTH, DEC_BATCH, CONV_CTX, CONV_W), f32),
        "page_table": page_table,
        "norm_in_g": 1.0 + 0.02 * jax.random.normal(ks[6], (DEPTH, D_MODEL), f32),
        "w_in": jax.random.normal(ks[7], (DEPTH, D_MODEL, N_IN), f32) * D_MODEL ** -0.5,
        "w_dw": jax.random.normal(ks[8], (DEPTH, CONV_WIDTH, CONV_W), f32) * CONV_WIDTH ** -0.5,
        "b_dw": 0.02 * jax.random.normal(ks[9], (DEPTH, CONV_W), f32),
        "ln_g": 1.0 + 0.02 * jax.random.normal(ks[10], (DEPTH, CONV_W), f32),
        "ln_b": 0.02 * jax.random.normal(ks[11], (DEPTH, CONV_W), f32),
        "w_pw2": jax.random.normal(ks[12], (DEPTH, CONV_W, CONV_W), f32) * CONV_W ** -0.5,
        "w_out": jax.random.normal(ks[13], (DEPTH, MIX_W, D_MODEL), f32) * MIX_W ** -0.5,
        "norm_f_g": 1.0 + 0.02 * jax.random.normal(ks[14], (D_MODEL,), f32),
    }


def reference(x_prompt, x_sample, cache_k, cache_v, state_conv, page_table,
              norm_in_g, w_in, w_dw, b_dw, ln_g, ln_b, w_pw2, w_out, norm_f_g):
    xp, xs = x_prompt, x_sample
    seq_p = xp.shape[1]
    n_pages = page_table.shape[1]
    past_len = n_pages * PAGE_SIZE
    pos_p = jnp.arange(seq_p, dtype=jnp.int32)
    pos_s = past_len + jnp.arange(xs.shape[1], dtype=jnp.int32)
    kp_l, vp_l, cp_l, ks_l, vs_l, cs_l = [], [], [], [], [], []
    for l in range(DEPTH):
        hp = rms_norm(xp, norm_in_g[l])
        hs = rms_norm(xs, norm_in_g[l])
        qp, kp, vp, gap, cvp, cgp, gcp = split_proj(hp, w_in[l])
        qs, ks, vs, gas, cvs, cgs, gcs = split_proj(hs, w_in[l])

        attn_p = lax.map(lambda a: moba_attend_sequence(a[0], a[1], a[2], pos_p), (qp, kp, vp))

        def one_seq(a, l=l):
            q1, k1, v1, pt = a
            k_past = cache_k[l, pt].reshape(past_len, N_HEADS, HEAD_DIM).astype(k1.dtype)
            v_past = cache_v[l, pt].reshape(past_len, N_HEADS, HEAD_DIM).astype(v1.dtype)
            k_full = jnp.concatenate([k_past, k1], axis=0)
            v_full = jnp.concatenate([v_past, v1], axis=0)
            return moba_attend_sequence(q1, k_full, v_full, pos_s)
        attn_s = lax.map(one_seq, (qs, ks, vs, page_table))

        ctx_p = jnp.zeros((xp.shape[0], CONV_CTX, CONV_W), dtype=cvp.dtype)
        conv_p, new_cp = conv_branch(cvp, cgp, ctx_p, w_dw[l], b_dw[l], ln_g[l], ln_b[l], w_pw2[l])
        conv_s, new_cs = conv_branch(cvs, cgs, state_conv[l], w_dw[l], b_dw[l], ln_g[l], ln_b[l], w_pw2[l])

        xp = xp + mix_out(attn_p, gap, conv_p, gcp, w_out[l])
        xs = xs + mix_out(attn_s, gas, conv_s, gcs, w_out[l])
        kp_l.append(kp); vp_l.append(vp); cp_l.append(new_cp)
        ks_l.append(ks); vs_l.append(vs); cs_l.append(new_cs)

    y_prompt = rms_norm(xp, norm_f_g)
    y_sample = rms_norm(xs, norm_f_g)
    return (y_prompt, y_sample, jnp.stack(kp_l), jnp.stack(vp_l), jnp.stack(cp_l),
            jnp.stack(ks_l), jnp.stack(vs_l), jnp.stack(cs_l))
```

```python
import functools
import math

import jax
import jax.numpy as jnp
from jax import lax
from jax.experimental import pallas as pl
from jax.experimental.pallas import tpu as pltpu

MOBA_BLOCK = 256
MOBA_TOP_K = 3
RMS_EPS = 1e-6
LN_EPS = 1e-5
LOG2E = math.log2(math.e)
NEG_BIG = -1e30
POS_BIG = 1e30
CONV_HALO = 32
V7X_SCOPED_VMEM_BYTES = 60000 * 1024
F32 = jnp.float32
BF16 = jnp.bfloat16


def _params(semantics, vmem_bytes):
    limit = int(min(V7X_SCOPED_VMEM_BYTES, max(vmem_bytes, 16 * 1024 * 1024)))
    return pltpu.CompilerParams(dimension_semantics=semantics, vmem_limit_bytes=limit)


def _tile(n, pref, unit):
    if n <= pref:
        return n
    t = (pref // unit) * unit
    while t >= unit:
        if n % t == 0:
            return t
        t -= unit
    raise ValueError(f"no tile for {n} (pref {pref}, unit {unit})")


def _silu(x):
    return x * jax.nn.sigmoid(x)


def _rms_kernel(x_ref, g_ref, o_ref):
    x = x_ref[...]
    ms = jnp.mean(x * x, axis=-1, keepdims=True)
    y = x * lax.rsqrt(ms + RMS_EPS)
    o_ref[...] = (y * g_ref[...]).astype(o_ref.dtype)


def _rms_norm(x, g, out_dtype):
    m, d = x.shape
    tm = _tile(m, 256, 8)
    blk = tm * d * 4
    return pl.pallas_call(
        _rms_kernel,
        grid=(m // tm,),
        in_specs=[pl.BlockSpec((tm, d), lambda i: (i, 0)), pl.BlockSpec((1, d), lambda i: (0, 0))],
        out_specs=pl.BlockSpec((tm, d), lambda i: (i, 0)),
        out_shape=jax.ShapeDtypeStruct((m, d), out_dtype),
        compiler_params=_params(("parallel",), 6 * blk),
        name="rms_norm",
    )(x, g.reshape(1, d))


def _mm_kernel(*refs, gated):
    if gated:
        a_ref, b_ref, g_ref, o_ref = refs
    else:
        a_ref, b_ref, o_ref = refs
    acc = jnp.dot(a_ref[...].astype(BF16), b_ref[...], preferred_element_type=F32)
    if gated:
        acc = acc * _silu(g_ref[...])
    o_ref[...] = acc.astype(o_ref.dtype)


def _matmul(a, b, col0, n, out_dtype, gate=None, gate_col0=0):
    m, k = a.shape
    tm = _tile(m, 512, 8)
    tn = _tile(n, 512, 128)
    assert col0 % tn == 0 and gate_col0 % tn == 0
    cb, gb = col0 // tn, gate_col0 // tn
    in_specs = [pl.BlockSpec((tm, k), lambda i, j: (i, 0)), pl.BlockSpec((k, tn), lambda i, j: (0, j + cb))]
    args = [a, b]
    if gate is not None:
        in_specs.append(pl.BlockSpec((tm, tn), lambda i, j: (i, j + gb)))
        args.append(gate)
    vmem = 2 * (tm * k * a.dtype.itemsize + k * tn * 2 + 3 * tm * tn * 4) + tm * k * 2
    return pl.pallas_call(
        functools.partial(_mm_kernel, gated=gate is not None),
        grid=(m // tm, n // tn),
        in_specs=in_specs,
        out_specs=pl.BlockSpec((tm, tn), lambda i, j: (i, j)),
        out_shape=jax.ShapeDtypeStruct((m, n), out_dtype),
        compiler_params=_params(("parallel", "parallel"), vmem),
        name="matmul",
    )(*args)


def _outproj_kernel(a1_ref, a2_ref, w1_ref, w2_ref, x_ref, o_ref):
    acc = jnp.dot(a1_ref[...].astype(BF16), w1_ref[...], preferred_element_type=F32)
    acc += jnp.dot(a2_ref[...].astype(BF16), w2_ref[...], preferred_element_type=F32)
    o_ref[...] = x_ref[...] + acc


def _outproj(a1, a2, w1, w2, x):
    m, k1 = a1.shape
    k2 = a2.shape[1]
    n = x.shape[1]
    tm = _tile(m, 512, 8)
    tn = _tile(n, 512, 128)
    vmem = 2 * (tm * k1 * a1.dtype.itemsize + tm * k2 * a2.dtype.itemsize + (k1 + k2) * tn * 2 + 3 * tm * tn * 4)
    return pl.pallas_call(
        _outproj_kernel,
        grid=(m // tm, n // tn),
        in_specs=[
            pl.BlockSpec((tm, k1), lambda i, j: (i, 0)),
            pl.BlockSpec((tm, k2), lambda i, j: (i, 0)),
            pl.BlockSpec((k1, tn), lambda i, j: (0, j)),
            pl.BlockSpec((k2, tn), lambda i, j: (0, j)),
            pl.BlockSpec((tm, tn), lambda i, j: (i, j)),
        ],
        out_specs=pl.BlockSpec((tm, tn), lambda i, j: (i, j)),
        out_shape=jax.ShapeDtypeStruct((m, n), F32),
        compiler_params=_params(("parallel", "parallel"), vmem),
        name="outproj",
    )(a1, a2, w1, w2, x)


def _topk_first_indices(gate, blk):
    nb = gate.shape[1]
    firsts = []
    for _ in range(MOBA_TOP_K):
        mx = jnp.max(gate, axis=-1, keepdims=True)
        first = jnp.min(jnp.where(gate == mx, blk, float(nb)), axis=-1, keepdims=True)
        firsts.append(first)
        gate = jnp.where(blk == first, -jnp.inf, gate)
    return firsts


def _kmean_prompt_kernel(k_ref, o_ref):
    o_ref[0] = jnp.sum(k_ref[...], axis=0, keepdims=True) * (1.0 / MOBA_BLOCK)


def _kmean_prompt(z, col0, w):
    t = z.shape[0]
    nb = t // MOBA_BLOCK
    assert col0 % w == 0
    return pl.pallas_call(
        _kmean_prompt_kernel,
        grid=(nb,),
        in_specs=[pl.BlockSpec((MOBA_BLOCK, w), lambda i: (i, col0 // w))],
        out_specs=pl.BlockSpec((1, 1, w), lambda i: (i, 0, 0)),
        out_shape=jax.ShapeDtypeStruct((nb, 1, w), F32),
        compiler_params=_params(("parallel",), 4 * MOBA_BLOCK * w * 4),
        name="kmean_prompt",
    )(z)


def _moba_prompt_kernel(q_ref, k_ref, v_ref, km_ref, ga_ref, o_ref, m_sc, l_sc, acc_sc, *, scale):
    i = pl.program_id(1)
    bq, hd = q_ref.shape
    nb = km_ref.shape[0]
    q = q_ref[...]
    gate = lax.dot_general(q, km_ref[:, 0, :], (((1,), (1,)), ((), ())),
                           precision=lax.Precision.HIGHEST, preferred_element_type=F32)
    blk = lax.broadcasted_iota(jnp.int32, (bq, nb), 1).astype(F32)
    past = blk < i.astype(F32)
    firsts = _topk_first_indices(jnp.where(past, gate, -jnp.inf), blk)
    sel = jnp.zeros((bq, nb), F32)
    for first in firsts:
        sel = jnp.maximum(sel, jnp.where(blk == first, 1.0, 0.0))
    sel = jnp.where(past, sel, 0.0)

    qb = (q * (scale * LOG2E)).astype(BF16)
    m_sc[...] = jnp.full(m_sc.shape, NEG_BIG, F32)
    l_sc[...] = jnp.zeros(l_sc.shape, F32)
    acc_sc[...] = jnp.zeros(acc_sc.shape, F32)

    def past_block(n, carry):
        off = pl.multiple_of(n * MOBA_BLOCK, MOBA_BLOCK)
        kb = k_ref[pl.ds(off, MOBA_BLOCK), :].astype(BF16)
        vb = v_ref[pl.ds(off, MOBA_BLOCK), :].astype(BF16)
        s = lax.dot_general(qb, kb, (((1,), (1,)), ((), ())), preferred_element_type=F32)
        picked = jnp.max(jnp.where(blk == n.astype(F32), sel, 0.0), axis=-1, keepdims=True) > 0.5
        m_old = m_sc[...]
        m_new = jnp.where(picked, jnp.maximum(m_old, jnp.max(s, axis=-1, keepdims=True)), m_old)
        alpha = jnp.exp2(m_old - m_new)
        p = jnp.exp2(s - jnp.where(picked, m_new, POS_BIG))
        l_sc[...] = alpha * l_sc[...] + jnp.sum(p, axis=-1, keepdims=True)
        acc_sc[...] = alpha * acc_sc[...] + jnp.dot(p.astype(BF16), vb, preferred_element_type=F32)
        m_sc[...] = m_new
        return carry

    lax.fori_loop(0, i, past_block, 0)

    off = pl.multiple_of(i * MOBA_BLOCK, MOBA_BLOCK)
    kb = k_ref[pl.ds(off, MOBA_BLOCK), :].astype(BF16)
    vb = v_ref[pl.ds(off, MOBA_BLOCK), :].astype(BF16)
    s = lax.dot_general(qb, kb, (((1,), (1,)), ((), ())), preferred_element_type=F32)
    row = lax.broadcasted_iota(jnp.int32, s.shape, 0)
    col = lax.broadcasted_iota(jnp.int32, s.shape, 1)
    s = jnp.where(col <= row, s, NEG_BIG)
    m_old = m_sc[...]
    m_new = jnp.maximum(m_old, jnp.max(s, axis=-1, keepdims=True))
    alpha = jnp.exp2(m_old - m_new)
    p = jnp.exp2(s - m_new)
    l = alpha * l_sc[...] + jnp.sum(p, axis=-1, keepdims=True)
    acc = alpha * acc_sc[...] + jnp.dot(p.astype(BF16), vb, preferred_element_type=F32)
    o_ref[...] = ((acc / l) * _silu(ga_ref[...])).astype(o_ref.dtype)


def _moba_prompt(z, kmean, n_heads, hd, q_col0, k_col0, v_col0, ga_col0):
    t = z.shape[0]
    assert t % MOBA_BLOCK == 0
    nb = t // MOBA_BLOCK
    qc, kc, vc, gc = (c // hd for c in (q_col0, k_col0, v_col0, ga_col0))
    vmem = 2 * (2 * t * hd * 4 + 3 * MOBA_BLOCK * hd * 4 + nb * 8 * hd * 4) + 16 * MOBA_BLOCK * MOBA_BLOCK * 4
    return pl.pallas_call(
        functools.partial(_moba_prompt_kernel, scale=hd ** -0.5),
        grid=(n_heads, nb),
        in_specs=[
            pl.BlockSpec((MOBA_BLOCK, hd), lambda h, i: (i, qc + h)),
            pl.BlockSpec((t, hd), lambda h, i: (0, kc + h)),
            pl.BlockSpec((t, hd), lambda h, i: (0, vc + h)),
            pl.BlockSpec((nb, 1, hd), lambda h, i: (0, 0, h)),
            pl.BlockSpec((MOBA_BLOCK, hd), lambda h, i: (i, gc + h)),
        ],
        out_specs=pl.BlockSpec((MOBA_BLOCK, hd), lambda h, i: (i, h)),
        out_shape=jax.ShapeDtypeStruct((t, n_heads * hd), BF16),
        scratch_shapes=[
            pltpu.VMEM((MOBA_BLOCK, 1), F32),
            pltpu.VMEM((MOBA_BLOCK, 1), F32),
            pltpu.VMEM((MOBA_BLOCK, hd), F32),
        ],
        compiler_params=_params(("parallel", "arbitrary"), vmem),
        name="moba_prompt",
    )(z, z, z, kmean, z)


def _conv_prompt_kernel(cv_ref, cvp_ref, cg_ref, cgp_ref, ctx_ref, w_ref, b_ref, lg_ref, lb_ref,
                        u_ref, tail_ref, gbuf):
    i = pl.program_id(0)
    tt = cv_ref.shape[0]
    cw = w_ref.shape[0]
    glu = cv_ref[...] * jax.nn.sigmoid(cg_ref[...])
    gbuf[pl.ds(CONV_HALO, tt), :] = glu

    @pl.when(i == 0)
    def _():
        gbuf[pl.ds(0, CONV_HALO), :] = ctx_ref[...]

    @pl.when(i > 0)
    def _():
        gbuf[pl.ds(0, CONV_HALO), :] = cvp_ref[...] * jax.nn.sigmoid(cgp_ref[...])

    base = CONV_HALO - (cw - 1)
    acc = w_ref[0:1, :] * gbuf[pl.ds(base, tt), :]
    for k in range(1, cw):
        acc = acc + w_ref[k:k + 1, :] * gbuf[pl.ds(base + k, tt), :]
    y = acc + b_ref[...]
    mu = jnp.mean(y, axis=-1, keepdims=True)
    yc = y - mu
    var = jnp.mean(yc * yc, axis=-1, keepdims=True)
    yn = yc * lax.rsqrt(var + LN_EPS) * lg_ref[...] + lb_ref[...]
    u_ref[...] = _silu(yn).astype(u_ref.dtype)
    tail_ref[...] = gbuf[pl.ds(tt, CONV_HALO), :]


def _conv_prompt(z, cv_col0, cg_col0, c, ctx, w_dw, b_dw, ln_g, ln_b):
    t = z.shape[0]
    cw = w_dw.shape[0]
    n_ctx = cw - 1
    assert n_ctx <= CONV_HALO
    tt = _tile(t, 256, CONV_HALO)
    assert t % tt == 0 and tt % CONV_HALO == 0
    r = tt // CONV_HALO
    cvb, cgb = cv_col0 // c, cg_col0 // c
    ctx_pad = jnp.concatenate([jnp.zeros((CONV_HALO - n_ctx, c), F32), ctx], axis=0)
    prev = lambda i: jnp.maximum(i * r - 1, 0)
    vmem = 2 * (2 * tt * c * 4 + 3 * CONV_HALO * c * 4 + tt * c * 2 + (cw + 3) * c * 4) + 6 * (tt + CONV_HALO) * c * 4
    u, tail = pl.pallas_call(
        _conv_prompt_kernel,
        grid=(t // tt,),
        in_specs=[
            pl.BlockSpec((tt, c), lambda i: (i, cvb)),
            pl.BlockSpec((CONV_HALO, c), lambda i: (prev(i), cvb)),
            pl.BlockSpec((tt, c), lambda i: (i, cgb)),
            pl.BlockSpec((CONV_HALO, c), lambda i: (prev(i), cgb)),
            pl.BlockSpec((CONV_HALO, c), lambda i: (0, 0)),
            pl.BlockSpec((cw, c), lambda i: (0, 0)),
            pl.BlockSpec((1, c), lambda i: (0, 0)),
            pl.BlockSpec((1, c), lambda i: (0, 0)),
            pl.BlockSpec((1, c), lambda i: (0, 0)),
        ],
        out_specs=[pl.BlockSpec((tt, c), lambda i: (i, 0)), pl.BlockSpec((CONV_HALO, c), lambda i: (0, 0))],
        out_shape=[jax.ShapeDtypeStruct((t, c), BF16), jax.ShapeDtypeStruct((CONV_HALO, c), F32)],
        scratch_shapes=[pltpu.VMEM((CONV_HALO + tt, c), F32)],
        compiler_params=_params(("arbitrary",), vmem),
        name="conv_prompt",
    )(z, z, z, z, ctx_pad, w_dw, b_dw.reshape(1, c), ln_g.reshape(1, c), ln_b.reshape(1, c))
    return u, tail[CONV_HALO - n_ctx:]


def _kmean_sample_kernel(pt_ref, *refs, pages_per_block):
    del pt_ref
    page_refs, o_ref = refs[:-1], refs[-1]
    n_heads, hd = page_refs[0].shape[-2:]
    for j in range(len(page_refs) // pages_per_block):
        tot = jnp.sum(page_refs[j * pages_per_block][0, 0], axis=0)
        for r in range(1, pages_per_block):
            tot = tot + jnp.sum(page_refs[j * pages_per_block + r][0, 0], axis=0)
        mean = tot * (1.0 / MOBA_BLOCK)
        for h in range(n_heads):
            o_ref[0, j, :, pl.ds(h * hd, hd)] = mean[h:h + 1, :]


def _kmean_sample(cache_k, layer, page_table):
    _, _, page, n_heads, hd = cache_k.shape
    bsz, n_pages = page_table.shape
    ppb = MOBA_BLOCK // page
    nbp = n_pages // ppb
    blocks_per_step = 2 if nbp % 2 == 0 else 1
    pps = ppb * blocks_per_step

    def page_spec(r):
        return pl.BlockSpec((1, 1, page, n_heads, hd), lambda b, n, pt: (layer, pt[b, n * pps + r], 0, 0, 0))

    return pl.pallas_call(
        functools.partial(_kmean_sample_kernel, pages_per_block=ppb),
        grid_spec=pltpu.PrefetchScalarGridSpec(
            num_scalar_prefetch=1,
            grid=(bsz, nbp // blocks_per_step),
            in_specs=[page_spec(r) for r in range(pps)],
            out_specs=pl.BlockSpec((1, blocks_per_step, 1, n_heads * hd), lambda b, n, pt: (b, n, 0, 0)),
        ),
        out_shape=jax.ShapeDtypeStruct((bsz, nbp, 1, n_heads * hd), F32),
        compiler_params=_params(("parallel", "arbitrary"), 3 * pps * page * n_heads * hd * 4),
        name="kmean_sample",
    )(page_table, *([cache_k] * pps))


def _gate_sample_kernel(q_ref, km_ref, o_ref, *, n_heads, hd):
    s = q_ref.shape[1]
    km = km_ref[0, :, 0, :]
    nbp, w = km.shape
    rows = [jnp.broadcast_to(q_ref[0, t:t + 1, :], (n_heads, w)) for t in range(s)]
    qbd = jnp.concatenate(rows, axis=0)
    r = lax.broadcasted_iota(jnp.int32, qbd.shape, 0) % n_heads
    c = lax.broadcasted_iota(jnp.int32, qbd.shape, 1) // hd
    qbd = jnp.where(r == c, qbd, 0.0)
    gate = lax.dot_general(qbd, km, (((1,), (1,)), ((), ())),
                           precision=lax.Precision.HIGHEST, preferred_element_type=F32)
    blk = lax.broadcasted_iota(jnp.int32, gate.shape, 1).astype(F32)
    firsts = _topk_first_indices(gate, blk)
    lane = lax.broadcasted_iota(jnp.int32, o_ref.shape[1:], 1)
    out = jnp.zeros(o_ref.shape[1:], F32)
    for j, first in enumerate(firsts):
        out = jnp.where(lane == j, first, out)
    o_ref[0] = out.astype(jnp.int32)


def _gate_sample(zs3, kmean, n_heads, hd, q_col0):
    bsz, s, _ = zs3.shape
    nbp = kmean.shape[1]
    w = n_heads * hd
    assert nbp >= MOBA_TOP_K and n_heads % 8 == 0 and q_col0 % w == 0
    return pl.pallas_call(
        functools.partial(_gate_sample_kernel, n_heads=n_heads, hd=hd),
        grid=(bsz,),
        in_specs=[
            pl.BlockSpec((1, s, w), lambda b: (b, 0, q_col0 // w)),
            pl.BlockSpec((1, nbp, 1, w), lambda b: (b, 0, 0, 0)),
        ],
        out_specs=pl.BlockSpec((1, s * n_heads, 128), lambda b: (b, 0, 0)),
        out_shape=jax.ShapeDtypeStruct((bsz, s * n_heads, 128), jnp.int32),
        compiler_params=_params(("parallel",), 8 * (s * n_heads + nbp * 8) * w * 4),
        name="gate_sample",
    )(zs3, kmean)


def _moba_sample_kernel(pt_ref, sel_ref, q_ref, kn_ref, vn_ref, ga_ref, ck_ref, cv_ref, o_ref,
                        kbuf, vbuf, sem, *, layer, scale, n_heads):
    b = pl.program_id(0)
    h = pl.program_id(1)
    n_b = pl.num_programs(0)
    s_len, hd = q_ref.shape[1:]
    page = ck_ref.shape[2]
    ppb = MOBA_BLOCK // page
    step = b * n_heads + h
    n_steps = n_b * n_heads

    def copies(bb, hh, slot):
        out = []
        for t in range(s_len):
            for j in range(MOBA_TOP_K):
                blk = sel_ref[((bb * s_len + t) * n_heads + hh) * MOBA_TOP_K + j]
                for r in range(ppb):
                    pg = pt_ref[bb, blk * ppb + r]
                    dst = pl.ds(((t * MOBA_TOP_K + j) * ppb + r) * page, page)
                    out.append(pltpu.make_async_copy(ck_ref.at[layer, pg, :, hh, :], kbuf.at[slot, dst, :],
                                                     sem.at[slot, 0]))
                    out.append(pltpu.make_async_copy(cv_ref.at[layer, pg, :, hh, :], vbuf.at[slot, dst, :],
                                                     sem.at[slot, 1]))
        return out

    @pl.when(step == 0)
    def _():
        for cp in copies(b, h, 0):
            cp.start()

    @pl.when(step + 1 < n_steps)
    def _():
        nxt = step + 1
        for cp in copies(nxt // n_heads, nxt % n_heads, nxt % 2):
            cp.start()

    slot = step % 2
    for cp in copies(b, h, slot):
        cp.wait()

    q = q_ref[0] * (scale * LOG2E)
    kk = kbuf[slot]
    vv = vbuf[slot]
    s = lax.dot_general(q, kk, (((1,), (1,)), ((), ())), preferred_element_type=F32)
    owner = lax.broadcasted_iota(jnp.int32, s.shape, 1) // (MOBA_TOP_K * MOBA_BLOCK)
    s = jnp.where(owner == lax.broadcasted_iota(jnp.int32, s.shape, 0), s, NEG_BIG)
    s_new = lax.dot_general(q, kn_ref[0], (((1,), (1,)), ((), ())), preferred_element_type=F32)
    causal = lax.broadcasted_iota(jnp.int32, s_new.shape, 1) <= lax.broadcasted_iota(jnp.int32, s_new.shape, 0)
    s_new = jnp.where(causal, s_new, NEG_BIG)
    m = jnp.maximum(jnp.max(s, axis=-1, keepdims=True), jnp.max(s_new, axis=-1, keepdims=True))
    p = jnp.exp2(s - m)
    p_new = jnp.exp2(s_new - m)
    l = jnp.sum(p, axis=-1, keepdims=True) + jnp.sum(p_new, axis=-1, keepdims=True)
    acc = jnp.dot(p, vv, preferred_element_type=F32) + jnp.dot(p_new, vn_ref[0], preferred_element_type=F32)
    o_ref[0] = (acc / l) * _silu(ga_ref[0])


def _moba_sample(zs3, sel_flat, page_table, cache_k, cache_v, layer, n_heads, hd, q_col0, k_col0, v_col0, ga_col0):
    bsz, s_len, _ = zs3.shape
    page = cache_k.shape[2]
    assert MOBA_BLOCK % page == 0 and (page_table.shape[1] * page) % MOBA_BLOCK == 0 and s_len <= MOBA_BLOCK
    rows = s_len * MOBA_TOP_K * MOBA_BLOCK
    qc, kc, vc, gc = (c // hd for c in (q_col0, k_col0, v_col0, ga_col0))
    tok = lambda c0: pl.BlockSpec((1, s_len, hd), lambda b, h, pt, sel: (b, 0, c0 + h))
    return pl.pallas_call(
        functools.partial(_moba_sample_kernel, layer=layer, scale=hd ** -0.5, n_heads=n_heads),
        grid_spec=pltpu.PrefetchScalarGridSpec(
            num_scalar_prefetch=2,
            grid=(bsz, n_heads),
            in_specs=[tok(qc), tok(kc), tok(vc), tok(gc),
                      pl.BlockSpec(memory_space=pl.ANY), pl.BlockSpec(memory_space=pl.ANY)],
            out_specs=pl.BlockSpec((1, s_len, hd), lambda b, h, pt, sel: (b, 0, h)),
            scratch_shapes=[
                pltpu.VMEM((2, rows, hd), F32),
                pltpu.VMEM((2, rows, hd), F32),
                pltpu.SemaphoreType.DMA((2, 2)),
            ],
        ),
        out_shape=jax.ShapeDtypeStruct((bsz, s_len, n_heads * hd), F32),
        compiler_params=_params(("arbitrary", "arbitrary"), 4 * rows * hd * 4 + 8 * rows * 8 * 4),
        name="moba_sample",
    )(page_table, sel_flat, zs3, zs3, zs3, zs3, cache_k, cache_v)


def _conv_sample_kernel(cv_ref, cg_ref, st_ref, w_ref, b_ref, lg_ref, lb_ref, u_ref, nst_ref, xp, ybuf):
    s_len = cv_ref.shape[1]
    n_ctx = st_ref.shape[1]
    cw = w_ref.shape[0]
    xp[pl.ds(0, n_ctx), :] = st_ref[0]
    xp[pl.ds(n_ctx, s_len), :] = cv_ref[0] * jax.nn.sigmoid(cg_ref[0])
    w = w_ref[...]
    for t in range(s_len):
        ybuf[pl.ds(t, 1), :] = jnp.sum(w * xp[pl.ds(t, cw), :], axis=0, keepdims=True)
    y = ybuf[pl.ds(0, s_len), :] + b_ref[...]
    mu = jnp.mean(y, axis=-1, keepdims=True)
    yc = y - mu
    var = jnp.mean(yc * yc, axis=-1, keepdims=True)
    yn = yc * lax.rsqrt(var + LN_EPS) * lg_ref[...] + lb_ref[...]
    u_ref[0] = _silu(yn)
    nst_ref[0] = xp[pl.ds(s_len, n_ctx), :]


def _conv_sample(zs3, cv_col0, cg_col0, state, w_dw, b_dw, ln_g, ln_b):
    bsz, s_len, _ = zs3.shape
    n_ctx, c = state.shape[1:]
    cw = w_dw.shape[0]
    assert n_ctx == cw - 1
    rows = -(-(n_ctx + s_len) // 8) * 8
    return pl.pallas_call(
        _conv_sample_kernel,
        grid=(bsz,),
        in_specs=[
            pl.BlockSpec((1, s_len, c), lambda b: (b, 0, cv_col0 // c)),
            pl.BlockSpec((1, s_len, c), lambda b: (b, 0, cg_col0 // c)),
            pl.BlockSpec((1, n_ctx, c), lambda b: (b, 0, 0)),
            pl.BlockSpec((cw, c), lambda b: (0, 0)),
            pl.BlockSpec((1, c), lambda b: (0, 0)),
            pl.BlockSpec((1, c), lambda b: (0, 0)),
            pl.BlockSpec((1, c), lambda b: (0, 0)),
        ],
        out_specs=[pl.BlockSpec((1, s_len, c), lambda b: (b, 0, 0)), pl.BlockSpec((1, n_ctx, c), lambda b: (b, 0, 0))],
        out_shape=[jax.ShapeDtypeStruct((bsz, s_len, c), F32), jax.ShapeDtypeStruct((bsz, n_ctx, c), F32)],
        scratch_shapes=[pltpu.VMEM((rows, c), F32), pltpu.VMEM((8, c), F32)],
        compiler_params=_params(("parallel",), 16 * rows * c * 4),
        name="conv_sample",
    )(zs3, zs3, state, w_dw, b_dw.reshape(1, c), ln_g.reshape(1, c), ln_b.reshape(1, c))


def kernel(x_prompt, x_sample, cache_k, cache_v, state_conv, page_table, norm_in_g, w_in, w_dw, b_dw, ln_g, ln_b,
           w_pw2, w_out, norm_f_g):
    bp, t, d = x_prompt.shape
    bs, s_len, _ = x_sample.shape
    depth = w_in.shape[0]
    n_heads, hd = cache_k.shape[3:]
    att_w = n_heads * hd
    c = state_conv.shape[-1]
    n_ctx = state_conv.shape[2]
    assert att_w == c, "column-block addressing of the combined projection assumes equal group widths"
    q0, k0, v0, ga0 = 0, att_w, 2 * att_w, 3 * att_w
    cv0, cg0, gc0 = 4 * att_w, 4 * att_w + c, 4 * att_w + 2 * c

    xp = x_prompt.reshape(bp * t, d)
    xs = x_sample.reshape(bs * s_len, d)
    outs = {name: [] for name in ("kp", "vp", "cp", "ks", "vs", "cs")}
    for l in range(depth):
        w_in_b = w_in[l].astype(BF16)
        w_pw2_b = w_pw2[l].astype(BF16)
        w_out_b = w_out[l].astype(BF16)
        w_out_att, w_out_conv = w_out_b[:att_w], w_out_b[att_w:]

        hp = _rms_norm(xp, norm_in_g[l], BF16)
        zp = _matmul(hp, w_in_b, 0, w_in_b.shape[1], F32)
        z_att, z_conv, kp, vp, cp = [], [], [], [], []
        for bi in range(bp):
            zb = zp[bi * t:(bi + 1) * t] if bp > 1 else zp
            kmean = _kmean_prompt(zb, k0, att_w)
            z_att.append(_moba_prompt(zb, kmean, n_heads, hd, q0, k0, v0, ga0))
            u, tail = _conv_prompt(zb, cv0, cg0, c, jnp.zeros((n_ctx, c), F32), w_dw[l], b_dw[l], ln_g[l], ln_b[l])
            z_conv.append(_matmul(u, w_pw2_b, 0, c, BF16, gate=zb, gate_col0=gc0))
            kp.append(zb[:, k0:k0 + att_w].reshape(t, n_heads, hd))
            vp.append(zb[:, v0:v0 + att_w].reshape(t, n_heads, hd))
            cp.append(tail)
        z_att = z_att[0] if bp == 1 else jnp.concatenate(z_att, axis=0)
        z_conv = z_conv[0] if bp == 1 else jnp.concatenate(z_conv, axis=0)
        xp = _outproj(z_att, z_conv, w_out_att, w_out_conv, xp)
        outs["kp"].append(jnp.stack(kp)); outs["vp"].append(jnp.stack(vp)); outs["cp"].append(jnp.stack(cp))

        hs = _rms_norm(xs, norm_in_g[l], BF16)
        zs = _matmul(hs, w_in_b, 0, w_in_b.shape[1], F32)
        zs3 = zs.reshape(bs, s_len, zs.shape[1])
        kmean_s = _kmean_sample(cache_k, l, page_table)
        sel = _gate_sample(zs3, kmean_s, n_heads, hd, q0)[:, :, :MOBA_TOP_K].reshape(-1)
        z_att_s = _moba_sample(zs3, sel, page_table, cache_k, cache_v, l, n_heads, hd, q0, k0, v0, ga0)
        u_s, new_state = _conv_sample(zs3, cv0, cg0, state_conv[l], w_dw[l], b_dw[l], ln_g[l], ln_b[l])
        z_conv_s = _matmul(u_s.reshape(bs * s_len, c), w_pw2_b, 0, c, BF16, gate=zs, gate_col0=gc0)
        xs = _outproj(z_att_s.reshape(bs * s_len, att_w), z_conv_s, w_out_att, w_out_conv, xs)
        outs["ks"].append(zs[:, k0:k0 + att_w].reshape(bs, s_len, n_heads, hd))
        outs["vs"].append(zs[:, v0:v0 + att_w].reshape(bs, s_len, n_heads, hd))
        outs["cs"].append(new_state)

    y_prompt = _rms_norm(xp, norm_f_g, F32).reshape(bp, t, d)
    y_sample = _rms_norm(xs, norm_f_g, F32).reshape(bs, s_len, d)
    return (y_prompt, y_sample, jnp.stack(outs["kp"]), jnp.stack(outs["vp"]), jnp.stack(outs["cp"]),
            jnp.stack(outs["ks"]), jnp.stack(outs["vs"]), jnp.stack(outs["cs"]))
```

```python
import functools
import math

import jax
import jax.numpy as jnp
from jax import lax
from jax.experimental import pallas as pl
from jax.experimental.pallas import tpu as pltpu

MOBA_BLOCK = 256
MOBA_TOP_K = 3
RMS_EPS = 1e-6
LN_EPS = 1e-5
LOG2E = math.log2(math.e)
NEG_BIG = -1e30
POS_BIG = 1e30
CONV_HALO = 32
SUBLANES = 8
CONV_ROWS = 32
CONV_LANES = 256
ONES_ROWS = 16
PROMPT_UNROLL = 8
PROMPT_QBLOCKS = 2
V7X_SCOPED_VMEM_BYTES = 60000 * 1024
F32 = jnp.float32
BF16 = jnp.bfloat16


def _params(semantics, vmem_bytes):
    limit = int(min(V7X_SCOPED_VMEM_BYTES, max(vmem_bytes, 16 * 1024 * 1024)))
    return pltpu.CompilerParams(dimension_semantics=semantics, vmem_limit_bytes=limit)


def _tile(n, pref, unit):
    if n <= pref:
        return n
    t = (pref // unit) * unit
    while t >= unit:
        if n % t == 0:
            return t
        t -= unit
    raise ValueError(f"no tile for {n} (pref {pref}, unit {unit})")


def _silu(x):
    return x * jax.nn.sigmoid(x)


def _rms_kernel(x_ref, g_ref, o_ref):
    x = x_ref[...]
    ms = jnp.mean(x * x, axis=-1, keepdims=True)
    y = x * lax.rsqrt(ms + RMS_EPS)
    o_ref[...] = (y * g_ref[...]).astype(o_ref.dtype)


def _rms_norm(x, g, out_dtype):
    m, d = x.shape
    tm = _tile(m, 256, 8)
    blk = tm * d * 4
    return pl.pallas_call(
        _rms_kernel,
        grid=(m // tm,),
        in_specs=[pl.BlockSpec((tm, d), lambda i: (i, 0)), pl.BlockSpec((1, d), lambda i: (0, 0))],
        out_specs=pl.BlockSpec((tm, d), lambda i: (i, 0)),
        out_shape=jax.ShapeDtypeStruct((m, d), out_dtype),
        compiler_params=_params(("parallel",), 6 * blk),
        name="rms_norm",
    )(x, g.reshape(1, d))


def _mm_kernel(*refs, gated):
    if gated:
        a_ref, b_ref, g_ref, o_ref = refs
    else:
        a_ref, b_ref, o_ref = refs
    acc = jnp.dot(a_ref[...].astype(BF16), b_ref[...], preferred_element_type=F32)
    if gated:
        acc = acc * _silu(g_ref[...])
    o_ref[...] = acc.astype(o_ref.dtype)


def _matmul(a, b, col0, n, out_dtype, gate=None, gate_col0=0):
    m, k = a.shape
    tm = _tile(m, 512, 8)
    tn = _tile(n, 512, 128)
    assert col0 % tn == 0 and gate_col0 % tn == 0
    cb, gb = col0 // tn, gate_col0 // tn
    in_specs = [pl.BlockSpec((tm, k), lambda i, j: (i, 0)), pl.BlockSpec((k, tn), lambda i, j: (0, j + cb))]
    args = [a, b]
    if gate is not None:
        in_specs.append(pl.BlockSpec((tm, tn), lambda i, j: (i, j + gb)))
        args.append(gate)
    vmem = 2 * (tm * k * a.dtype.itemsize + k * tn * 2 + 3 * tm * tn * 4) + tm * k * 2
    return pl.pallas_call(
        functools.partial(_mm_kernel, gated=gate is not None),
        grid=(m // tm, n // tn),
        in_specs=in_specs,
        out_specs=pl.BlockSpec((tm, tn), lambda i, j: (i, j)),
        out_shape=jax.ShapeDtypeStruct((m, n), out_dtype),
        compiler_params=_params(("parallel", "parallel"), vmem),
        name="matmul",
    )(*args)


def _outproj_kernel(a1_ref, a2_ref, w1_ref, w2_ref, x_ref, o_ref):
    acc = jnp.dot(a1_ref[...].astype(BF16), w1_ref[...], preferred_element_type=F32)
    acc += jnp.dot(a2_ref[...].astype(BF16), w2_ref[...], preferred_element_type=F32)
    o_ref[...] = x_ref[...] + acc


def _outproj(a1, a2, w1, w2, x):
    m, k1 = a1.shape
    k2 = a2.shape[1]
    n = x.shape[1]
    tm = _tile(m, 512, 8)
    tn = _tile(n, 512, 128)
    vmem = 2 * (tm * k1 * a1.dtype.itemsize + tm * k2 * a2.dtype.itemsize + (k1 + k2) * tn * 2 + 3 * tm * tn * 4)
    return pl.pallas_call(
        _outproj_kernel,
        grid=(m // tm, n // tn),
        in_specs=[
            pl.BlockSpec((tm, k1), lambda i, j: (i, 0)),
            pl.BlockSpec((tm, k2), lambda i, j: (i, 0)),
            pl.BlockSpec((k1, tn), lambda i, j: (0, j)),
            pl.BlockSpec((k2, tn), lambda i, j: (0, j)),
            pl.BlockSpec((tm, tn), lambda i, j: (i, j)),
        ],
        out_specs=pl.BlockSpec((tm, tn), lambda i, j: (i, j)),
        out_shape=jax.ShapeDtypeStruct((m, n), F32),
        compiler_params=_params(("parallel", "parallel"), vmem),
        name="outproj",
    )(a1, a2, w1, w2, x)


def _topk_first_indices(gate, blk, axis=-1):
    nb = gate.shape[axis]
    firsts = []
    for _ in range(MOBA_TOP_K):
        mx = jnp.max(gate, axis=axis, keepdims=True)
        first = jnp.min(jnp.where(gate == mx, blk, float(nb)), axis=axis, keepdims=True)
        firsts.append(first)
        gate = jnp.where(blk == first, -jnp.inf, gate)
    return firsts


def _kmean_prompt_kernel(k_ref, o_ref):
    o_ref[0] = jnp.sum(k_ref[...], axis=0, keepdims=True) * (1.0 / MOBA_BLOCK)


def _kmean_prompt(z, col0, w):
    t = z.shape[0]
    nb = t // MOBA_BLOCK
    assert col0 % w == 0
    return pl.pallas_call(
        _kmean_prompt_kernel,
        grid=(nb,),
        in_specs=[pl.BlockSpec((MOBA_BLOCK, w), lambda i: (i, col0 // w))],
        out_specs=pl.BlockSpec((1, 1, w), lambda i: (i, 0, 0)),
        out_shape=jax.ShapeDtypeStruct((nb, 1, w), F32),
        compiler_params=_params(("parallel",), 4 * MOBA_BLOCK * w * 4),
        name="kmean_prompt",
    )(z)


def _gate_prompt_kernel(q_ref, km_ref, o_ref):
    j = pl.program_id(1)
    tq = q_ref.shape[0]
    nb = km_ref.shape[0]
    gate = lax.dot_general(km_ref[:, 0, :], q_ref[...], (((1,), (1,)), ((), ())),
                           precision=lax.Precision.HIGHEST, preferred_element_type=F32)
    blk = lax.broadcasted_iota(jnp.int32, (nb, tq), 0)
    own = (lax.broadcasted_iota(jnp.int32, (nb, tq), 1) + j * tq) // MOBA_BLOCK
    past = blk < own
    blk_f = blk.astype(F32)
    firsts = _topk_first_indices(jnp.where(past, gate, -jnp.inf), blk_f, axis=0)
    sel = jnp.zeros((nb, tq), F32)
    for first in firsts:
        sel = jnp.maximum(sel, jnp.where(blk_f == first, 1.0, 0.0))
    sel = jnp.where(past, sel, 0.0)
    o_ref[0] = jnp.where(blk == own, 1.0, sel)


def _gate_prompt(z, kmean, n_heads, hd, q_col0):
    t = z.shape[0]
    nb = kmean.shape[0]
    tq = _tile(t, 2048, MOBA_BLOCK)
    return pl.pallas_call(
        _gate_prompt_kernel,
        grid=(n_heads, t // tq),
        in_specs=[
            pl.BlockSpec((tq, hd), lambda h, j: (j, q_col0 // hd + h)),
            pl.BlockSpec((nb, 1, hd), lambda h, j: (0, 0, h)),
        ],
        out_specs=pl.BlockSpec((1, nb, tq), lambda h, j: (h, 0, j)),
        out_shape=jax.ShapeDtypeStruct((n_heads, nb, t), F32),
        compiler_params=_params(("parallel", "parallel"), 4 * tq * hd * 4 + 24 * nb * tq * 4),
        name="gate_prompt",
    )(z, kmean)


def _moba_prompt_kernel(q_ref, k_ref, v_ref, sel_ref, ga_ref, o_ref, kb_sc, vt_sc, bias_sc, s_sc, bm_sc, acc_sc,
                        m_sc, *, scale):
    i = pl.program_id(1)
    bq, hd = q_ref.shape
    qblocks = bq // MOBA_BLOCK
    n_chunk_total = vt_sc.shape[0]
    nt = (((1,), (1,)), ((), ()))
    chunk = PROMPT_UNROLL * MOBA_BLOCK

    @pl.when(i == 0)
    def _():
        kb_sc[...] = k_ref[...].astype(BF16)
        key = lax.broadcasted_iota(jnp.int32, (MOBA_BLOCK, bq), 0)
        qry = lax.broadcasted_iota(jnp.int32, (MOBA_BLOCK, bq), 1)
        bias_sc[0] = jnp.zeros((MOBA_BLOCK, bq), F32)
        for r in range(qblocks):
            bias_sc[1 + r] = jnp.where(qry - r * MOBA_BLOCK >= key, 0.0, NEG_BIG)

        def stage(c, carry):
            for u in range(PROMPT_UNROLL):
                off = pl.multiple_of((c * PROMPT_UNROLL + u) * MOBA_BLOCK, MOBA_BLOCK)
                vt_sc[c, pl.ds(0, hd), pl.ds(u * MOBA_BLOCK, MOBA_BLOCK)] = (
                    v_ref[pl.ds(off, MOBA_BLOCK), :].T.astype(BF16))
            vt_sc[c, pl.ds(hd, ONES_ROWS), :] = jnp.ones((ONES_ROWS, chunk), BF16)
            return carry

        lax.fori_loop(0, n_chunk_total, stage, 0)

    qb = (q_ref[...] * (scale * LOG2E)).astype(BF16)
    first_own = i * qblocks
    n_chunks = (first_own + qblocks - 1) // PROMPT_UNROLL + 1

    def scores_into(slot, c):
        cc = jnp.minimum(c, n_chunks - 1)
        off = pl.multiple_of(cc * chunk, chunk)
        s = lax.dot_general(kb_sc[pl.ds(off, chunk), :], qb, nt, preferred_element_type=F32)
        for u in range(PROMPT_UNROLL):
            r = cc * PROMPT_UNROLL + u - first_own
            which = jnp.where((r >= 0) & (r < qblocks), r + 1, 0)
            su = s[u * MOBA_BLOCK:(u + 1) * MOBA_BLOCK] + bias_sc[which]
            s_sc[slot, pl.ds(u * MOBA_BLOCK, MOBA_BLOCK), :] = su
            bm_sc[slot, pl.ds(u, 1), :] = jnp.max(su, axis=0, keepdims=True)

    scores_into(0, 0)
    acc_sc[...] = jnp.zeros(acc_sc.shape, F32)
    m_sc[...] = jnp.full(m_sc.shape, NEG_BIG, F32)

    def trip(c, carry):
        cur = c % 2
        m_old = m_sc[...]
        picked = sel_ref[0, pl.ds(pl.multiple_of(c * PROMPT_UNROLL, PROMPT_UNROLL), PROMPT_UNROLL), :] > 0.5
        m_new = jnp.maximum(m_old, jnp.max(jnp.where(picked, bm_sc[cur], NEG_BIG), axis=0, keepdims=True))
        ps = []
        for u in range(PROMPT_UNROLL):
            m_eff = jnp.where(picked[u:u + 1], m_new, POS_BIG)
            ps.append(jnp.exp2(s_sc[cur, pl.ds(u * MOBA_BLOCK, MOBA_BLOCK), :] - m_eff).astype(BF16))
        pv = jnp.dot(vt_sc[c], jnp.concatenate(ps, axis=0), preferred_element_type=F32)
        acc_sc[...] = acc_sc[...] * jnp.exp2(m_old - m_new) + pv
        m_sc[...] = m_new
        scores_into(1 - cur, c + 1)
        return carry

    lax.fori_loop(0, n_chunks, trip, 0)

    acc = acc_sc[...]
    out = (acc[:hd] / acc[hd:hd + 1]).T
    o_ref[...] = (out * _silu(ga_ref[...])).astype(o_ref.dtype)


def _moba_prompt(z, sel, n_heads, hd, q_col0, k_col0, v_col0, ga_col0):
    t = z.shape[0]
    chunk = PROMPT_UNROLL * MOBA_BLOCK
    bq = PROMPT_QBLOCKS * MOBA_BLOCK
    assert t % chunk == 0 and t % bq == 0
    nb = t // MOBA_BLOCK
    qc, kc, vc, gc = (c // hd for c in (q_col0, k_col0, v_col0, ga_col0))
    scratch = [
        pltpu.VMEM((t, hd), BF16),
        pltpu.VMEM((t // chunk, hd + ONES_ROWS, chunk), BF16),
        pltpu.VMEM((1 + PROMPT_QBLOCKS, MOBA_BLOCK, bq), F32),
        pltpu.VMEM((2, chunk, bq), F32),
        pltpu.VMEM((2, PROMPT_UNROLL, bq), F32),
        pltpu.VMEM((hd + ONES_ROWS, bq), F32),
        pltpu.VMEM((1, bq), F32),
    ]
    vmem = (2 * (2 * t * hd * 4 + 3 * bq * hd * 4 + nb * bq * 4)
            + t * hd * 2 + (t // chunk) * (hd + ONES_ROWS) * chunk * 2 + (1 + PROMPT_QBLOCKS) * MOBA_BLOCK * bq * 4
            + 5 * chunk * bq * 4)
    return pl.pallas_call(
        functools.partial(_moba_prompt_kernel, scale=hd ** -0.5),
        grid=(n_heads, t // bq),
        in_specs=[
            pl.BlockSpec((bq, hd), lambda h, i: (i, qc + h)),
            pl.BlockSpec((t, hd), lambda h, i: (0, kc + h)),
            pl.BlockSpec((t, hd), lambda h, i: (0, vc + h)),
            pl.BlockSpec((1, nb, bq), lambda h, i: (h, 0, i)),
            pl.BlockSpec((bq, hd), lambda h, i: (i, gc + h)),
        ],
        out_specs=pl.BlockSpec((bq, hd), lambda h, i: (i, h)),
        out_shape=jax.ShapeDtypeStruct((t, n_heads * hd), BF16),
        scratch_shapes=scratch,
        compiler_params=_params(("parallel", "arbitrary"), vmem),
        name="moba_prompt",
    )(z, z, z, sel, z)


def _conv_prompt_kernel(cv_ref, cvp_ref, cg_ref, cgp_ref, ctx_ref, w_ref, b_ref, lg_ref, lb_ref,
                        u_ref, tail_ref, gbuf, shifted, ybuf, wb):
    i = pl.program_id(0)
    tt = cv_ref.shape[0]
    cw = w_ref.shape[0]
    glu = cv_ref[...] * jax.nn.sigmoid(cg_ref[...])
    gbuf[pl.ds(CONV_HALO, tt), :] = glu

    @pl.when(i == 0)
    def _():
        gbuf[pl.ds(0, CONV_HALO), :] = ctx_ref[...]

    @pl.when(i > 0)
    def _():
        gbuf[pl.ds(0, CONV_HALO), :] = cvp_ref[...] * jax.nn.sigmoid(cgp_ref[...])

    span = shifted.shape[1]
    for s in range(1, SUBLANES):
        shifted[s - 1] = gbuf[pl.ds(s, span), :]
    base = CONV_HALO - (cw - 1)

    @pl.when(i == 0)
    def _():
        for k in range(cw):
            wb[k] = jnp.broadcast_to(w_ref[k:k + 1, :], wb.shape[1:])

    def rows_chunk(j, carry):
        r0 = pl.multiple_of(j * CONV_ROWS, CONV_ROWS)
        for c0 in range(0, ybuf.shape[1], CONV_LANES):
            lanes = pl.ds(c0, CONV_LANES)
            acc = None
            for s in range(SUBLANES):
                wholes = [w for w in range((base + cw - 1) // SUBLANES + 1) if 0 <= w * SUBLANES + s - base < cw]
                rows = pl.ds(r0 + wholes[0] * SUBLANES, CONV_ROWS + (wholes[-1] - wholes[0]) * SUBLANES)
                xs = gbuf[rows, lanes] if s == 0 else shifted[s - 1, rows, lanes]
                for w in wholes:
                    k = w * SUBLANES + s - base
                    x = xs[(w - wholes[0]) * SUBLANES:(w - wholes[0]) * SUBLANES + CONV_ROWS]
                    wk = jnp.concatenate([wb[k, :, lanes]] * (CONV_ROWS // SUBLANES), axis=0)
                    acc = wk * x if acc is None else acc + wk * x
            ybuf[pl.ds(r0, CONV_ROWS), lanes] = acc
        return carry

    lax.fori_loop(0, tt // CONV_ROWS, rows_chunk, 0)
    y = ybuf[...] + b_ref[...]
    mu = jnp.mean(y, axis=-1, keepdims=True)
    yc = y - mu
    var = jnp.mean(yc * yc, axis=-1, keepdims=True)
    yn = yc * lax.rsqrt(var + LN_EPS) * lg_ref[...] + lb_ref[...]
    u_ref[...] = _silu(yn).astype(u_ref.dtype)
    tail_ref[...] = gbuf[pl.ds(tt, CONV_HALO), :]


def _conv_prompt(z, cv_col0, cg_col0, c, ctx, w_dw, b_dw, ln_g, ln_b):
    t = z.shape[0]
    cw = w_dw.shape[0]
    n_ctx = cw - 1
    assert n_ctx <= CONV_HALO
    tt = _tile(t, 256, CONV_HALO)
    assert t % tt == 0 and tt % CONV_HALO == 0
    r = tt // CONV_HALO
    cvb, cgb = cv_col0 // c, cg_col0 // c
    ctx_pad = jnp.concatenate([jnp.zeros((CONV_HALO - n_ctx, c), F32), ctx], axis=0)
    prev = lambda i: jnp.maximum(i * r - 1, 0)
    assert tt % CONV_ROWS == 0 and CONV_ROWS % SUBLANES == 0 and CONV_HALO % SUBLANES == 0
    span = tt + CONV_HALO - SUBLANES
    vmem = (2 * (2 * tt * c * 4 + 3 * CONV_HALO * c * 4 + tt * c * 2 + (cw + 3) * c * 4)
            + ((tt + CONV_HALO) + (SUBLANES - 1) * span + 5 * tt) * c * 4)
    u, tail = pl.pallas_call(
        _conv_prompt_kernel,
        grid=(t // tt,),
        in_specs=[
            pl.BlockSpec((tt, c), lambda i: (i, cvb)),
            pl.BlockSpec((CONV_HALO, c), lambda i: (prev(i), cvb)),
            pl.BlockSpec((tt, c), lambda i: (i, cgb)),
            pl.BlockSpec((CONV_HALO, c), lambda i: (prev(i), cgb)),
            pl.BlockSpec((CONV_HALO, c), lambda i: (0, 0)),
            pl.BlockSpec((cw, c), lambda i: (0, 0)),
            pl.BlockSpec((1, c), lambda i: (0, 0)),
            pl.BlockSpec((1, c), lambda i: (0, 0)),
            pl.BlockSpec((1, c), lambda i: (0, 0)),
        ],
        out_specs=[pl.BlockSpec((tt, c), lambda i: (i, 0)), pl.BlockSpec((CONV_HALO, c), lambda i: (0, 0))],
        out_shape=[jax.ShapeDtypeStruct((t, c), BF16), jax.ShapeDtypeStruct((CONV_HALO, c), F32)],
        scratch_shapes=[
            pltpu.VMEM((CONV_HALO + tt, c), F32),
            pltpu.VMEM((SUBLANES - 1, span, c), F32),
            pltpu.VMEM((tt, c), F32),
            pltpu.VMEM((cw, SUBLANES, c), F32),
        ],
        compiler_params=_params(("arbitrary",), vmem),
        name="conv_prompt",
    )(z, z, z, z, ctx_pad, w_dw, b_dw.reshape(1, c), ln_g.reshape(1, c), ln_b.reshape(1, c))
    return u, tail[CONV_HALO - n_ctx:]


def _kmean_sample_kernel(pt_ref, *refs, pages_per_block):
    del pt_ref
    page_refs, o_ref = refs[:-1], refs[-1]
    n_heads, hd = page_refs[0].shape[-2:]
    for j in range(len(page_refs) // pages_per_block):
        tot = jnp.sum(page_refs[j * pages_per_block][0, 0], axis=0)
        for r in range(1, pages_per_block):
            tot = tot + jnp.sum(page_refs[j * pages_per_block + r][0, 0], axis=0)
        mean = tot * (1.0 / MOBA_BLOCK)
        for h in range(n_heads):
            o_ref[0, j, :, pl.ds(h * hd, hd)] = mean[h:h + 1, :]


def _kmean_sample(cache_k, layer, page_table):
    _, _, page, n_heads, hd = cache_k.shape
    bsz, n_pages = page_table.shape
    ppb = MOBA_BLOCK // page
    nbp = n_pages // ppb
    blocks_per_step = 2 if nbp % 2 == 0 else 1
    pps = ppb * blocks_per_step

    def page_spec(r):
        return pl.BlockSpec((1, 1, page, n_heads, hd), lambda b, n, pt: (layer, pt[b, n * pps + r], 0, 0, 0))

    return pl.pallas_call(
        functools.partial(_kmean_sample_kernel, pages_per_block=ppb),
        grid_spec=pltpu.PrefetchScalarGridSpec(
            num_scalar_prefetch=1,
            grid=(bsz, nbp // blocks_per_step),
            in_specs=[page_spec(r) for r in range(pps)],
            out_specs=pl.BlockSpec((1, blocks_per_step, 1, n_heads * hd), lambda b, n, pt: (b, n, 0, 0)),
        ),
        out_shape=jax.ShapeDtypeStruct((bsz, nbp, 1, n_heads * hd), F32),
        compiler_params=_params(("parallel", "arbitrary"), 3 * pps * page * n_heads * hd * 4),
        name="kmean_sample",
    )(page_table, *([cache_k] * pps))


def _gate_sample_kernel(q_ref, km_ref, o_ref, *, n_heads, hd):
    s = q_ref.shape[1]
    km = km_ref[0, :, 0, :]
    nbp, w = km.shape
    rows = [jnp.broadcast_to(q_ref[0, t:t + 1, :], (n_heads, w)) for t in range(s)]
    qbd = jnp.concatenate(rows, axis=0)
    r = lax.broadcasted_iota(jnp.int32, qbd.shape, 0) % n_heads
    c = lax.broadcasted_iota(jnp.int32, qbd.shape, 1) // hd
    qbd = jnp.where(r == c, qbd, 0.0)
    gate = lax.dot_general(qbd, km, (((1,), (1,)), ((), ())),
                           precision=lax.Precision.HIGHEST, preferred_element_type=F32)
    blk = lax.broadcasted_iota(jnp.int32, gate.shape, 1).astype(F32)
    firsts = _topk_first_indices(gate, blk)
    lane = lax.broadcasted_iota(jnp.int32, o_ref.shape[1:], 1)
    out = jnp.zeros(o_ref.shape[1:], F32)
    for j, first in enumerate(firsts):
        out = jnp.where(lane == j, first, out)
    o_ref[0] = out.astype(jnp.int32)


def _gate_sample(zs3, kmean, n_heads, hd, q_col0):
    bsz, s, _ = zs3.shape
    nbp = kmean.shape[1]
    w = n_heads * hd
    assert nbp >= MOBA_TOP_K and n_heads % 8 == 0 and q_col0 % w == 0
    return pl.pallas_call(
        functools.partial(_gate_sample_kernel, n_heads=n_heads, hd=hd),
        grid=(bsz,),
        in_specs=[
            pl.BlockSpec((1, s, w), lambda b: (b, 0, q_col0 // w)),
            pl.BlockSpec((1, nbp, 1, w), lambda b: (b, 0, 0, 0)),
        ],
        out_specs=pl.BlockSpec((1, s * n_heads, 128), lambda b: (b, 0, 0)),
        out_shape=jax.ShapeDtypeStruct((bsz, s * n_heads, 128), jnp.int32),
        compiler_params=_params(("parallel",), 8 * (s * n_heads + nbp * 8) * w * 4),
        name="gate_sample",
    )(zs3, kmean)


def _moba_sample_kernel(pt_ref, sel_ref, q_ref, kn_ref, vn_ref, ga_ref, ck_ref, cv_ref, o_ref,
                        kbuf, vbuf, sem, *, layer, scale, n_heads):
    b = pl.program_id(0)
    h = pl.program_id(1)
    n_b = pl.num_programs(0)
    s_len, hd = q_ref.shape[1:]
    page = ck_ref.shape[2]
    ppb = MOBA_BLOCK // page
    step = b * n_heads + h
    n_steps = n_b * n_heads

    def copies(bb, hh, slot):
        out = []
        for t in range(s_len):
            for j in range(MOBA_TOP_K):
                blk = sel_ref[((bb * s_len + t) * n_heads + hh) * MOBA_TOP_K + j]
                for r in range(ppb):
                    pg = pt_ref[bb, blk * ppb + r]
                    dst = pl.ds(((t * MOBA_TOP_K + j) * ppb + r) * page, page)
                    out.append(pltpu.make_async_copy(ck_ref.at[layer, pg, :, hh, :], kbuf.at[slot, dst, :],
                                                     sem.at[slot, 0]))
                    out.append(pltpu.make_async_copy(cv_ref.at[layer, pg, :, hh, :], vbuf.at[slot, dst, :],
                                                     sem.at[slot, 1]))
        return out

    @pl.when(step == 0)
    def _():
        for cp in copies(b, h, 0):
            cp.start()

    @pl.when(step + 1 < n_steps)
    def _():
        nxt = step + 1
        for cp in copies(nxt // n_heads, nxt % n_heads, nxt % 2):
            cp.start()

    slot = step % 2
    for cp in copies(b, h, slot):
        cp.wait()

    q = q_ref[0] * (scale * LOG2E)
    kk = kbuf[slot]
    vv = vbuf[slot]
    s = lax.dot_general(q, kk, (((1,), (1,)), ((), ())), preferred_element_type=F32)
    owner = lax.broadcasted_iota(jnp.int32, s.shape, 1) // (MOBA_TOP_K * MOBA_BLOCK)
    s = jnp.where(owner == lax.broadcasted_iota(jnp.int32, s.shape, 0), s, NEG_BIG)
    s_new = lax.dot_general(q, kn_ref[0], (((1,), (1,)), ((), ())), preferred_element_type=F32)
    causal = lax.broadcasted_iota(jnp.int32, s_new.shape, 1) <= lax.broadcasted_iota(jnp.int32, s_new.shape, 0)
    s_new = jnp.where(causal, s_new, NEG_BIG)
    m = jnp.maximum(jnp.max(s, axis=-1, keepdims=True), jnp.max(s_new, axis=-1, keepdims=True))
    p = jnp.exp2(s - m)
    p_new = jnp.exp2(s_new - m)
    l = jnp.sum(p, axis=-1, keepdims=True) + jnp.sum(p_new, axis=-1, keepdims=True)
    acc = jnp.dot(p, vv, preferred_element_type=F32) + jnp.dot(p_new, vn_ref[0], preferred_element_type=F32)
    o_ref[0] = (acc / l) * _silu(ga_ref[0])


def _moba_sample(zs3, sel_flat, page_table, cache_k, cache_v, layer, n_heads, hd, q_col0, k_col0, v_col0, ga_col0):
    bsz, s_len, _ = zs3.shape
    page = cache_k.shape[2]
    assert MOBA_BLOCK % page == 0 and (page_table.shape[1] * page) % MOBA_BLOCK == 0 and s_len <= MOBA_BLOCK
    rows = s_len * MOBA_TOP_K * MOBA_BLOCK
    qc, kc, vc, gc = (c // hd for c in (q_col0, k_col0, v_col0, ga_col0))
    tok = lambda c0: pl.BlockSpec((1, s_len, hd), lambda b, h, pt, sel: (b, 0, c0 + h))
    return pl.pallas_call(
        functools.partial(_moba_sample_kernel, layer=layer, scale=hd ** -0.5, n_heads=n_heads),
        grid_spec=pltpu.PrefetchScalarGridSpec(
            num_scalar_prefetch=2,
            grid=(bsz, n_heads),
            in_specs=[tok(qc), tok(kc), tok(vc), tok(gc),
                      pl.BlockSpec(memory_space=pl.ANY), pl.BlockSpec(memory_space=pl.ANY)],
            out_specs=pl.BlockSpec((1, s_len, hd), lambda b, h, pt, sel: (b, 0, h)),
            scratch_shapes=[
                pltpu.VMEM((2, rows, hd), F32),
                pltpu.VMEM((2, rows, hd), F32),
                pltpu.SemaphoreType.DMA((2, 2)),
            ],
        ),
        out_shape=jax.ShapeDtypeStruct((bsz, s_len, n_heads * hd), F32),
        compiler_params=_params(("arbitrary", "arbitrary"), 4 * rows * hd * 4 + 8 * rows * 8 * 4),
        name="moba_sample",
    )(page_table, sel_flat, zs3, zs3, zs3, zs3, cache_k, cache_v)


def _conv_sample_kernel(cv_ref, cg_ref, st_ref, w_ref, b_ref, lg_ref, lb_ref, u_ref, nst_ref, xp, ybuf):
    s_len = cv_ref.shape[1]
    n_ctx = st_ref.shape[1]
    cw = w_ref.shape[0]
    xp[pl.ds(0, n_ctx), :] = st_ref[0]
    xp[pl.ds(n_ctx, s_len), :] = cv_ref[0] * jax.nn.sigmoid(cg_ref[0])
    w = w_ref[...]
    for t in range(s_len):
        ybuf[pl.ds(t, 1), :] = jnp.sum(w * xp[pl.ds(t, cw), :], axis=0, keepdims=True)
    y = ybuf[pl.ds(0, s_len), :] + b_ref[...]
    mu = jnp.mean(y, axis=-1, keepdims=True)
    yc = y - mu
    var = jnp.mean(yc * yc, axis=-1, keepdims=True)
    yn = yc * lax.rsqrt(var + LN_EPS) * lg_ref[...] + lb_ref[...]
    u_ref[0] = _silu(yn)
    nst_ref[0] = xp[pl.ds(s_len, n_ctx), :]


def _conv_sample(zs3, cv_col0, cg_col0, state, w_dw, b_dw, ln_g, ln_b):
    bsz, s_len, _ = zs3.shape
    n_ctx, c = state.shape[1:]
    cw = w_dw.shape[0]
    assert n_ctx == cw - 1
    rows = -(-(n_ctx + s_len) // 8) * 8
    return pl.pallas_call(
        _conv_sample_kernel,
        grid=(bsz,),
        in_specs=[
            pl.BlockSpec((1, s_len, c), lambda b: (b, 0, cv_col0 // c)),
            pl.BlockSpec((1, s_len, c), lambda b: (b, 0, cg_col0 // c)),
            pl.BlockSpec((1, n_ctx, c), lambda b: (b, 0, 0)),
            pl.BlockSpec((cw, c), lambda b: (0, 0)),
            pl.BlockSpec((1, c), lambda b: (0, 0)),
            pl.BlockSpec((1, c), lambda b: (0, 0)),
            pl.BlockSpec((1, c), lambda b: (0, 0)),
        ],
        out_specs=[pl.BlockSpec((1, s_len, c), lambda b: (b, 0, 0)), pl.BlockSpec((1, n_ctx, c), lambda b: (b, 0, 0))],
        out_shape=[jax.ShapeDtypeStruct((bsz, s_len, c), F32), jax.ShapeDtypeStruct((bsz, n_ctx, c), F32)],
        scratch_shapes=[pltpu.VMEM((rows, c), F32), pltpu.VMEM((8, c), F32)],
        compiler_params=_params(("parallel",), 16 * rows * c * 4),
        name="conv_sample",
    )(zs3, zs3, state, w_dw, b_dw.reshape(1, c), ln_g.reshape(1, c), ln_b.reshape(1, c))


def kernel(x_prompt, x_sample, cache_k, cache_v, state_conv, page_table, norm_in_g, w_in, w_dw, b_dw, ln_g, ln_b,
           w_pw2, w_out, norm_f_g):
    bp, t, d = x_prompt.shape
    bs, s_len, _ = x_sample.shape
    depth = w_in.shape[0]
    n_heads, hd = cache_k.shape[3:]
    att_w = n_heads * hd
    c = state_conv.shape[-1]
    n_ctx = state_conv.shape[2]
    assert att_w == c, "column-block addressing of the combined projection assumes equal group widths"
    q0, k0, v0, ga0 = 0, att_w, 2 * att_w, 3 * att_w
    cv0, cg0, gc0 = 4 * att_w, 4 * att_w + c, 4 * att_w + 2 * c

    xp = x_prompt.reshape(bp * t, d)
    xs = x_sample.reshape(bs * s_len, d)
    outs = {name: [] for name in ("kp", "vp", "cp", "ks", "vs", "cs")}
    for l in range(depth):
        w_in_b = w_in[l].astype(BF16)
        w_pw2_b = w_pw2[l].astype(BF16)
        w_out_b = w_out[l].astype(BF16)
        w_out_att, w_out_conv = w_out_b[:att_w], w_out_b[att_w:]

        hp = _rms_norm(xp, norm_in_g[l], BF16)
        zp = _matmul(hp, w_in_b, 0, w_in_b.shape[1], F32)
        z_att, z_conv, kp, vp, cp = [], [], [], [], []
        for bi in range(bp):
            zb = zp[bi * t:(bi + 1) * t] if bp > 1 else zp
            kmean = _kmean_prompt(zb, k0, att_w)
            sel_p = _gate_prompt(zb, kmean, n_heads, hd, q0)
            z_att.append(_moba_prompt(zb, sel_p, n_heads, hd, q0, k0, v0, ga0))
            u, tail = _conv_prompt(zb, cv0, cg0, c, jnp.zeros((n_ctx, c), F32), w_dw[l], b_dw[l], ln_g[l], ln_b[l])
            z_conv.append(_matmul(u, w_pw2_b, 0, c, BF16, gate=zb, gate_col0=gc0))
            kp.append(zb[:, k0:k0 + att_w].reshape(t, n_heads, hd))
            vp.append(zb[:, v0:v0 + att_w].reshape(t, n_heads, hd))
            cp.append(tail)
        z_att = z_att[0] if bp == 1 else jnp.concatenate(z_att, axis=0)
        z_conv = z_conv[0] if bp == 1 else jnp.concatenate(z_conv, axis=0)
        xp = _outproj(z_att, z_conv, w_out_att, w_out_conv, xp)
        outs["kp"].append(jnp.stack(kp)); outs["vp"].append(jnp.stack(vp)); outs["cp"].append(jnp.stack(cp))

        hs = _rms_norm(xs, norm_in_g[l], BF16)
        zs = _matmul(hs, w_in_b, 0, w_in_b.shape[1], F32)
        zs3 = zs.reshape(bs, s_len, zs.shape[1])
        kmean_s = _kmean_sample(cache_k, l, page_table)
        sel = _gate_sample(zs3, kmean_s, n_heads, hd, q0)[:, :, :MOBA_TOP_K].reshape(-1)
        z_att_s = _moba_sample(zs3, sel, page_table, cache_k, cache_v, l, n_heads, hd, q0, k0, v0, ga0)
        u_s, new_state = _conv_sample(zs3, cv0, cg0, state_conv[l], w_dw[l], b_dw[l], ln_g[l], ln_b[l])
        z_conv_s = _matmul(u_s.reshape(bs * s_len, c), w_pw2_b, 0, c, BF16, gate=zs, gate_col0=gc0)
        xs = _outproj(z_att_s.reshape(bs * s_len, att_w), z_conv_s, w_out_att, w_out_conv, xs)
        outs["ks"].append(zs[:, k0:k0 + att_w].reshape(bs, s_len, n_heads, hd))
        outs["vs"].append(zs[:, v0:v0 + att_w].reshape(bs, s_len, n_heads, hd))
        outs["cs"].append(new_state)

    y_prompt = _rms_norm(xp, norm_f_g, F32).reshape(bp, t, d)
    y_sample = _rms_norm(xs, norm_f_g, F32).reshape(bs, s_len, d)
    return (y_prompt, y_sample, jnp.stack(outs["kp"]), jnp.stack(outs["vp"]), jnp.stack(outs["cp"]),
            jnp.stack(outs["ks"]), jnp.stack(outs["vs"]), jnp.stack(outs["cs"]))
```

```python
import functools
import math

import jax
import jax.numpy as jnp
from jax import lax
from jax.experimental import pallas as pl
from jax.experimental.pallas import tpu as pltpu

MOBA_BLOCK = 256
MOBA_TOP_K = 3
RMS_EPS = 1e-6
LN_EPS = 1e-5
LOG2E = math.log2(math.e)
NEG_BIG = -1e30
POS_BIG = 1e30
CONV_HALO = 32
SUBLANES = 8
CONV_ROWS = 32
CONV_LANES = 256
ONES_ROWS = 16
PROMPT_UNROLL = 8
PROMPT_QBLOCKS = 2
V7X_SCOPED_VMEM_BYTES = 60000 * 1024
F32 = jnp.float32
BF16 = jnp.bfloat16


def _params(semantics, vmem_bytes):
    limit = int(min(V7X_SCOPED_VMEM_BYTES, max(vmem_bytes, 16 * 1024 * 1024)))
    return pltpu.CompilerParams(dimension_semantics=semantics, vmem_limit_bytes=limit)


def _tile(n, pref, unit):
    if n <= pref:
        return n
    t = (pref // unit) * unit
    while t >= unit:
        if n % t == 0:
            return t
        t -= unit
    raise ValueError(f"no tile for {n} (pref {pref}, unit {unit})")


def _silu(x):
    return x * jax.nn.sigmoid(x)


def _rms_kernel(x_ref, g_ref, o_ref):
    x = x_ref[...]
    ms = jnp.mean(x * x, axis=-1, keepdims=True)
    y = x * lax.rsqrt(ms + RMS_EPS)
    o_ref[...] = (y * g_ref[...]).astype(o_ref.dtype)


def _rms_norm(x, g, out_dtype):
    m, d = x.shape
    tm = _tile(m, 256, 8)
    blk = tm * d * 4
    return pl.pallas_call(
        _rms_kernel,
        grid=(m // tm,),
        in_specs=[pl.BlockSpec((tm, d), lambda i: (i, 0)), pl.BlockSpec((1, d), lambda i: (0, 0))],
        out_specs=pl.BlockSpec((tm, d), lambda i: (i, 0)),
        out_shape=jax.ShapeDtypeStruct((m, d), out_dtype),
        compiler_params=_params(("parallel",), 6 * blk),
        name="rms_norm",
    )(x, g.reshape(1, d))


def _mm_kernel(*refs, gated):
    if gated:
        a_ref, b_ref, g_ref, o_ref = refs
    else:
        a_ref, b_ref, o_ref = refs
    acc = jnp.dot(a_ref[...].astype(BF16), b_ref[...], preferred_element_type=F32)
    if gated:
        acc = acc * _silu(g_ref[...])
    o_ref[...] = acc.astype(o_ref.dtype)


def _matmul(a, b, col0, n, out_dtype, gate=None, gate_col0=0):
    m, k = a.shape
    tm = _tile(m, 512, 8)
    tn = _tile(n, 512, 128)
    assert col0 % tn == 0 and gate_col0 % tn == 0
    cb, gb = col0 // tn, gate_col0 // tn
    in_specs = [pl.BlockSpec((tm, k), lambda i, j: (i, 0)), pl.BlockSpec((k, tn), lambda i, j: (0, j + cb))]
    args = [a, b]
    if gate is not None:
        in_specs.append(pl.BlockSpec((tm, tn), lambda i, j: (i, j + gb)))
        args.append(gate)
    vmem = 2 * (tm * k * a.dtype.itemsize + k * tn * 2 + 3 * tm * tn * 4) + tm * k * 2
    return pl.pallas_call(
        functools.partial(_mm_kernel, gated=gate is not None),
        grid=(m // tm, n // tn),
        in_specs=in_specs,
        out_specs=pl.BlockSpec((tm, tn), lambda i, j: (i, j)),
        out_shape=jax.ShapeDtypeStruct((m, n), out_dtype),
        compiler_params=_params(("parallel", "parallel"), vmem),
        name="matmul",
    )(*args)


def _mm_kmean_kernel(pt_ref, a_ref, b_ref, ck_ref, o_ref, km_ref, pbuf, sem, *,
                     layer, n_blocks, seq_blocks, blocks_per_step, n_j):
    step = pl.program_id(0) * n_j + pl.program_id(1)
    ppb = pbuf.shape[1] // blocks_per_step
    n_heads, hd = pbuf.shape[-2:]
    n_active = -(-n_blocks // blocks_per_step)

    def copies(st, slot):
        out = []
        for r in range(blocks_per_step):
            g = jnp.minimum(st * blocks_per_step + r, n_blocks - 1)
            for p in range(ppb):
                pg = pt_ref[g // seq_blocks, (g % seq_blocks) * ppb + p]
                out.append(pltpu.make_async_copy(ck_ref.at[layer, pg], pbuf.at[slot, r * ppb + p], sem.at[slot]))
        return out

    @pl.when(step == 0)
    def _():
        for cp in copies(0, 0):
            cp.start()

    @pl.when(step + 1 < n_active)
    def _():
        for cp in copies(step + 1, (step + 1) % 2):
            cp.start()

    slot = step % 2

    @pl.when(step < n_active)
    def _():
        for cp in copies(step, slot):
            cp.wait()

    means = []
    for r in range(blocks_per_step):
        tot = jnp.sum(pbuf[slot, r * ppb], axis=0)
        for p in range(1, ppb):
            tot = tot + jnp.sum(pbuf[slot, r * ppb + p], axis=0)
        means.append(tot * (1.0 / MOBA_BLOCK))
    o_ref[...] = jnp.dot(a_ref[...], b_ref[...], preferred_element_type=F32)

    @pl.when(step < n_active)
    def _():
        for r in range(blocks_per_step):
            for h in range(n_heads):
                km_ref[r, :, pl.ds(h * hd, hd)] = means[r][h:h + 1, :]


def _matmul_kmean(a, b, cache_k, layer, page_table):
    m, k = a.shape
    n = b.shape[1]
    _, _, page, n_heads, hd = cache_k.shape
    bsz, n_pages = page_table.shape
    ppb = MOBA_BLOCK // page
    seq_blocks = n_pages // ppb
    n_blocks = bsz * seq_blocks
    tm = _tile(m, 1024, 8)
    tn = _tile(n, 512, 128)
    n_i, n_j = m // tm, n // tn
    blocks_per_step = -(-n_blocks // (n_i * n_j))
    n_active = -(-n_blocks // blocks_per_step)
    pages = blocks_per_step * ppb
    vmem = 2 * (tm * k * 2 + k * tn * 2 + 2 * tm * tn * 4) + 2 * pages * page * n_heads * hd * 4
    z, km = pl.pallas_call(
        functools.partial(_mm_kmean_kernel, layer=layer, n_blocks=n_blocks, seq_blocks=seq_blocks,
                          blocks_per_step=blocks_per_step, n_j=n_j),
        grid_spec=pltpu.PrefetchScalarGridSpec(
            num_scalar_prefetch=1,
            grid=(n_i, n_j),
            in_specs=[
                pl.BlockSpec((tm, k), lambda i, j, pt: (i, 0)),
                pl.BlockSpec((k, tn), lambda i, j, pt: (0, j)),
                pl.BlockSpec(memory_space=pl.ANY),
            ],
            out_specs=[
                pl.BlockSpec((tm, tn), lambda i, j, pt: (i, j)),
                pl.BlockSpec((blocks_per_step, 1, n_heads * hd),
                             lambda i, j, pt: (jnp.minimum(i * n_j + j, n_active - 1), 0, 0)),
            ],
            scratch_shapes=[
                pltpu.VMEM((2, pages, page, n_heads, hd), F32),
                pltpu.SemaphoreType.DMA((2,)),
            ],
        ),
        out_shape=[
            jax.ShapeDtypeStruct((m, n), F32),
            jax.ShapeDtypeStruct((n_active * blocks_per_step, 1, n_heads * hd), F32),
        ],
        compiler_params=_params(("arbitrary", "arbitrary"), vmem),
        name="matmul_kmean",
    )(page_table, a, b, cache_k)
    return z, km[:n_blocks].reshape(bsz, seq_blocks, 1, n_heads * hd)


def _outproj_kernel(a1_ref, a2_ref, w1_ref, w2_ref, x_ref, o_ref):
    acc = jnp.dot(a1_ref[...].astype(BF16), w1_ref[...], preferred_element_type=F32)
    acc += jnp.dot(a2_ref[...].astype(BF16), w2_ref[...], preferred_element_type=F32)
    o_ref[...] = x_ref[...] + acc


def _outproj(a1, a2, w1, w2, x):
    m, k1 = a1.shape
    k2 = a2.shape[1]
    n = x.shape[1]
    tm = _tile(m, 512, 8)
    tn = _tile(n, 512, 128)
    vmem = 2 * (tm * k1 * a1.dtype.itemsize + tm * k2 * a2.dtype.itemsize + (k1 + k2) * tn * 2 + 3 * tm * tn * 4)
    return pl.pallas_call(
        _outproj_kernel,
        grid=(m // tm, n // tn),
        in_specs=[
            pl.BlockSpec((tm, k1), lambda i, j: (i, 0)),
            pl.BlockSpec((tm, k2), lambda i, j: (i, 0)),
            pl.BlockSpec((k1, tn), lambda i, j: (0, j)),
            pl.BlockSpec((k2, tn), lambda i, j: (0, j)),
            pl.BlockSpec((tm, tn), lambda i, j: (i, j)),
        ],
        out_specs=pl.BlockSpec((tm, tn), lambda i, j: (i, j)),
        out_shape=jax.ShapeDtypeStruct((m, n), F32),
        compiler_params=_params(("parallel", "parallel"), vmem),
        name="outproj",
    )(a1, a2, w1, w2, x)


def _topk_first_indices(gate, blk, axis=-1):
    nb = gate.shape[axis]
    firsts = []
    for _ in range(MOBA_TOP_K):
        mx = jnp.max(gate, axis=axis, keepdims=True)
        first = jnp.min(jnp.where(gate == mx, blk, float(nb)), axis=axis, keepdims=True)
        firsts.append(first)
        gate = jnp.where(blk == first, -jnp.inf, gate)
    return firsts


def _kmean_prompt_kernel(k_ref, o_ref):
    o_ref[0] = jnp.sum(k_ref[...], axis=0, keepdims=True) * (1.0 / MOBA_BLOCK)


def _kmean_prompt(z, col0, w):
    t = z.shape[0]
    nb = t // MOBA_BLOCK
    assert col0 % w == 0
    return pl.pallas_call(
        _kmean_prompt_kernel,
        grid=(nb,),
        in_specs=[pl.BlockSpec((MOBA_BLOCK, w), lambda i: (i, col0 // w))],
        out_specs=pl.BlockSpec((1, 1, w), lambda i: (i, 0, 0)),
        out_shape=jax.ShapeDtypeStruct((nb, 1, w), F32),
        compiler_params=_params(("parallel",), 4 * MOBA_BLOCK * w * 4),
        name="kmean_prompt",
    )(z)


def _gate_prompt_kernel(q_ref, km_ref, o_ref):
    j = pl.program_id(1)
    tq = q_ref.shape[0]
    nb = km_ref.shape[0]
    gate = lax.dot_general(km_ref[:, 0, :], q_ref[...], (((1,), (1,)), ((), ())),
                           precision=lax.Precision.HIGHEST, preferred_element_type=F32)
    blk = lax.broadcasted_iota(jnp.int32, (nb, tq), 0)
    own = (lax.broadcasted_iota(jnp.int32, (nb, tq), 1) + j * tq) // MOBA_BLOCK
    past = blk < own
    blk_f = blk.astype(F32)
    firsts = _topk_first_indices(jnp.where(past, gate, -jnp.inf), blk_f, axis=0)
    sel = jnp.zeros((nb, tq), F32)
    for first in firsts:
        sel = jnp.maximum(sel, jnp.where(blk_f == first, 1.0, 0.0))
    sel = jnp.where(past, sel, 0.0)
    o_ref[0] = jnp.where(blk == own, 1.0, sel)


def _gate_prompt(z, kmean, n_heads, hd, q_col0):
    t = z.shape[0]
    nb = kmean.shape[0]
    tq = _tile(t, 2048, MOBA_BLOCK)
    return pl.pallas_call(
        _gate_prompt_kernel,
        grid=(n_heads, t // tq),
        in_specs=[
            pl.BlockSpec((tq, hd), lambda h, j: (j, q_col0 // hd + h)),
            pl.BlockSpec((nb, 1, hd), lambda h, j: (0, 0, h)),
        ],
        out_specs=pl.BlockSpec((1, nb, tq), lambda h, j: (h, 0, j)),
        out_shape=jax.ShapeDtypeStruct((n_heads, nb, t), F32),
        compiler_params=_params(("parallel", "parallel"), 4 * tq * hd * 4 + 24 * nb * tq * 4),
        name="gate_prompt",
    )(z, kmean)


def _moba_prompt_kernel(q_ref, qn_ref, k_ref, v_ref, sel_ref, ga_ref, o_ref, kb_sc, vt_sc, bias_sc, s_sc, bm_sc,
                        acc_sc, m_sc, slot_sc, *, scale):
    i = pl.program_id(1)
    bq, hd = q_ref.shape
    qblocks = bq // MOBA_BLOCK
    n_chunk_total = vt_sc.shape[0]
    nt = (((1,), (1,)), ((), ()))
    chunk = PROMPT_UNROLL * MOBA_BLOCK

    @pl.when(i == 0)
    def _():
        kb_sc[...] = k_ref[...].astype(BF16)
        key = lax.broadcasted_iota(jnp.int32, (MOBA_BLOCK, bq), 0)
        qry = lax.broadcasted_iota(jnp.int32, (MOBA_BLOCK, bq), 1)
        bias_sc[0] = jnp.zeros((MOBA_BLOCK, bq), F32)
        for r in range(qblocks):
            bias_sc[1 + r] = jnp.where(qry - r * MOBA_BLOCK >= key, 0.0, NEG_BIG)

        def stage(c, carry):
            for u in range(PROMPT_UNROLL):
                off = pl.multiple_of((c * PROMPT_UNROLL + u) * MOBA_BLOCK, MOBA_BLOCK)
                vt_sc[c, pl.ds(0, hd), pl.ds(u * MOBA_BLOCK, MOBA_BLOCK)] = (
                    v_ref[pl.ds(off, MOBA_BLOCK), :].T.astype(BF16))
            vt_sc[c, pl.ds(hd, ONES_ROWS), :] = jnp.ones((ONES_ROWS, chunk), BF16)
            return carry

        lax.fori_loop(0, n_chunk_total, stage, 0)

    qb = (q_ref[...] * (scale * LOG2E)).astype(BF16)
    qb_next = (qn_ref[...] * (scale * LOG2E)).astype(BF16)
    first_own = i * qblocks
    first_own_next = jnp.minimum(i + 1, pl.num_programs(1) - 1) * qblocks
    n_chunks = (first_own + qblocks - 1) // PROMPT_UNROLL + 1

    def scores_into(slot, q_bf16, tile_first_own, c):
        off = pl.multiple_of(c * chunk, chunk)
        s = lax.dot_general(kb_sc[pl.ds(off, chunk), :], q_bf16, nt, preferred_element_type=F32)
        for u in range(PROMPT_UNROLL):
            r = c * PROMPT_UNROLL + u - tile_first_own
            which = jnp.where((r >= 0) & (r < qblocks), r + 1, 0)
            su = s[u * MOBA_BLOCK:(u + 1) * MOBA_BLOCK] + bias_sc[which]
            s_sc[slot, pl.ds(u * MOBA_BLOCK, MOBA_BLOCK), :] = su
            bm_sc[slot, pl.ds(u, 1), :] = jnp.max(su, axis=0, keepdims=True)

    @pl.when(i == 0)
    def _():
        slot_sc[0] = 0
        scores_into(0, qb, first_own, 0)

    slot0 = slot_sc[0]
    acc_sc[...] = jnp.zeros(acc_sc.shape, F32)
    m_sc[...] = jnp.full(m_sc.shape, NEG_BIG, F32)

    def trip(c, carry):
        cur = (slot0 + c) % 2
        m_old = m_sc[...]
        picked = sel_ref[0, pl.ds(pl.multiple_of(c * PROMPT_UNROLL, PROMPT_UNROLL), PROMPT_UNROLL), :] > 0.5
        m_new = jnp.maximum(m_old, jnp.max(jnp.where(picked, bm_sc[cur], NEG_BIG), axis=0, keepdims=True))
        ps = []
        for u in range(PROMPT_UNROLL):
            m_eff = jnp.where(picked[u:u + 1], m_new, POS_BIG)
            ps.append(jnp.exp2(s_sc[cur, pl.ds(u * MOBA_BLOCK, MOBA_BLOCK), :] - m_eff).astype(BF16))
        pv = jnp.dot(vt_sc[c], jnp.concatenate(ps, axis=0), preferred_element_type=F32)
        acc_sc[...] = acc_sc[...] * jnp.exp2(m_old - m_new) + pv
        m_sc[...] = m_new
        last = c + 1 == n_chunks
        scores_into(1 - cur, jnp.where(last, qb_next, qb), jnp.where(last, first_own_next, first_own),
                    jnp.where(last, 0, c + 1))
        return carry

    lax.fori_loop(0, n_chunks, trip, 0)
    slot_sc[0] = (slot0 + n_chunks) % 2

    acc = acc_sc[...]
    out = (acc[:hd] / acc[hd:hd + 1]).T
    o_ref[...] = (out * _silu(ga_ref[...])).astype(o_ref.dtype)


def _moba_prompt(z, sel, n_heads, hd, q_col0, k_col0, v_col0, ga_col0):
    t = z.shape[0]
    chunk = PROMPT_UNROLL * MOBA_BLOCK
    bq = PROMPT_QBLOCKS * MOBA_BLOCK
    assert t % chunk == 0 and t % bq == 0
    nb = t // MOBA_BLOCK
    qc, kc, vc, gc = (c // hd for c in (q_col0, k_col0, v_col0, ga_col0))
    scratch = [
        pltpu.VMEM((t, hd), BF16),
        pltpu.VMEM((t // chunk, hd + ONES_ROWS, chunk), BF16),
        pltpu.VMEM((1 + PROMPT_QBLOCKS, MOBA_BLOCK, bq), F32),
        pltpu.VMEM((2, chunk, bq), F32),
        pltpu.VMEM((2, PROMPT_UNROLL, bq), F32),
        pltpu.VMEM((hd + ONES_ROWS, bq), F32),
        pltpu.VMEM((1, bq), F32),
        pltpu.SMEM((1,), jnp.int32),
    ]
    n_tiles = t // bq
    vmem = (2 * (2 * t * hd * 4 + 4 * bq * hd * 4 + nb * bq * 4)
            + t * hd * 2 + (t // chunk) * (hd + ONES_ROWS) * chunk * 2 + (1 + PROMPT_QBLOCKS) * MOBA_BLOCK * bq * 4
            + 5 * chunk * bq * 4)
    return pl.pallas_call(
        functools.partial(_moba_prompt_kernel, scale=hd ** -0.5),
        grid=(n_heads, n_tiles),
        in_specs=[
            pl.BlockSpec((bq, hd), lambda h, i: (i, qc + h)),
            pl.BlockSpec((bq, hd), lambda h, i: (jnp.minimum(i + 1, n_tiles - 1), qc + h)),
            pl.BlockSpec((t, hd), lambda h, i: (0, kc + h)),
            pl.BlockSpec((t, hd), lambda h, i: (0, vc + h)),
            pl.BlockSpec((1, nb, bq), lambda h, i: (h, 0, i)),
            pl.BlockSpec((bq, hd), lambda h, i: (i, gc + h)),
        ],
        out_specs=pl.BlockSpec((bq, hd), lambda h, i: (i, h)),
        out_shape=jax.ShapeDtypeStruct((t, n_heads * hd), BF16),
        scratch_shapes=scratch,
        compiler_params=_params(("parallel", "arbitrary"), vmem),
        name="moba_prompt",
    )(z, z, z, z, sel, z)


def _conv_prompt_kernel(cv_ref, cvp_ref, cg_ref, cgp_ref, ctx_ref, w_ref, b_ref, lg_ref, lb_ref,
                        u_ref, tail_ref, gbuf, shifted, ybuf, wb):
    i = pl.program_id(0)
    tt = cv_ref.shape[0]
    cw = w_ref.shape[0]
    glu = cv_ref[...] * jax.nn.sigmoid(cg_ref[...])
    gbuf[pl.ds(CONV_HALO, tt), :] = glu

    @pl.when(i == 0)
    def _():
        gbuf[pl.ds(0, CONV_HALO), :] = ctx_ref[...]

    @pl.when(i > 0)
    def _():
        gbuf[pl.ds(0, CONV_HALO), :] = cvp_ref[...] * jax.nn.sigmoid(cgp_ref[...])

    span = shifted.shape[1]
    for s in range(1, SUBLANES):
        shifted[s - 1] = gbuf[pl.ds(s, span), :]
    base = CONV_HALO - (cw - 1)

    @pl.when(i == 0)
    def _():
        for k in range(cw):
            wb[k] = jnp.broadcast_to(w_ref[k:k + 1, :], wb.shape[1:])

    def rows_chunk(j, carry):
        r0 = pl.multiple_of(j * CONV_ROWS, CONV_ROWS)
        for c0 in range(0, ybuf.shape[1], CONV_LANES):
            lanes = pl.ds(c0, CONV_LANES)
            acc = None
            for s in range(SUBLANES):
                wholes = [w for w in range((base + cw - 1) // SUBLANES + 1) if 0 <= w * SUBLANES + s - base < cw]
                rows = pl.ds(r0 + wholes[0] * SUBLANES, CONV_ROWS + (wholes[-1] - wholes[0]) * SUBLANES)
                xs = gbuf[rows, lanes] if s == 0 else shifted[s - 1, rows, lanes]
                for w in wholes:
                    k = w * SUBLANES + s - base
                    x = xs[(w - wholes[0]) * SUBLANES:(w - wholes[0]) * SUBLANES + CONV_ROWS]
                    wk = jnp.concatenate([wb[k, :, lanes]] * (CONV_ROWS // SUBLANES), axis=0)
                    acc = wk * x if acc is None else acc + wk * x
            ybuf[pl.ds(r0, CONV_ROWS), lanes] = acc
        return carry

    lax.fori_loop(0, tt // CONV_ROWS, rows_chunk, 0)
    y = ybuf[...] + b_ref[...]
    mu = jnp.mean(y, axis=-1, keepdims=True)
    yc = y - mu
    var = jnp.mean(yc * yc, axis=-1, keepdims=True)
    yn = yc * lax.rsqrt(var + LN_EPS) * lg_ref[...] + lb_ref[...]
    u_ref[...] = _silu(yn).astype(u_ref.dtype)
    tail_ref[...] = gbuf[pl.ds(tt, CONV_HALO), :]


def _conv_prompt(z, cv_col0, cg_col0, c, ctx, w_dw, b_dw, ln_g, ln_b):
    t = z.shape[0]
    cw = w_dw.shape[0]
    n_ctx = cw - 1
    assert n_ctx <= CONV_HALO
    tt = _tile(t, 256, CONV_HALO)
    assert t % tt == 0 and tt % CONV_HALO == 0
    r = tt // CONV_HALO
    cvb, cgb = cv_col0 // c, cg_col0 // c
    ctx_pad = jnp.concatenate([jnp.zeros((CONV_HALO - n_ctx, c), F32), ctx], axis=0)
    prev = lambda i: jnp.maximum(i * r - 1, 0)
    assert tt % CONV_ROWS == 0 and CONV_ROWS % SUBLANES == 0 and CONV_HALO % SUBLANES == 0
    span = tt + CONV_HALO - SUBLANES
    vmem = (2 * (2 * tt * c * 4 + 3 * CONV_HALO * c * 4 + tt * c * 2 + (cw + 3) * c * 4)
            + ((tt + CONV_HALO) + (SUBLANES - 1) * span + 5 * tt) * c * 4)
    u, tail = pl.pallas_call(
        _conv_prompt_kernel,
        grid=(t // tt,),
        in_specs=[
            pl.BlockSpec((tt, c), lambda i: (i, cvb)),
            pl.BlockSpec((CONV_HALO, c), lambda i: (prev(i), cvb)),
            pl.BlockSpec((tt, c), lambda i: (i, cgb)),
            pl.BlockSpec((CONV_HALO, c), lambda i: (prev(i), cgb)),
            pl.BlockSpec((CONV_HALO, c), lambda i: (0, 0)),
            pl.BlockSpec((cw, c), lambda i: (0, 0)),
            pl.BlockSpec((1, c), lambda i: (0, 0)),
            pl.BlockSpec((1, c), lambda i: (0, 0)),
            pl.BlockSpec((1, c), lambda i: (0, 0)),
        ],
        out_specs=[pl.BlockSpec((tt, c), lambda i: (i, 0)), pl.BlockSpec((CONV_HALO, c), lambda i: (0, 0))],
        out_shape=[jax.ShapeDtypeStruct((t, c), BF16), jax.ShapeDtypeStruct((CONV_HALO, c), F32)],
        scratch_shapes=[
            pltpu.VMEM((CONV_HALO + tt, c), F32),
            pltpu.VMEM((SUBLANES - 1, span, c), F32),
            pltpu.VMEM((tt, c), F32),
            pltpu.VMEM((cw, SUBLANES, c), F32),
        ],
        compiler_params=_params(("arbitrary",), vmem),
        name="conv_prompt",
    )(z, z, z, z, ctx_pad, w_dw, b_dw.reshape(1, c), ln_g.reshape(1, c), ln_b.reshape(1, c))
    return u, tail[CONV_HALO - n_ctx:]


def _gate_sample_kernel(q_ref, km_ref, o_ref, *, n_heads, hd):
    s = q_ref.shape[1]
    km = km_ref[0, :, 0, :]
    nbp, w = km.shape
    rows = [jnp.broadcast_to(q_ref[0, t:t + 1, :], (n_heads, w)) for t in range(s)]
    qbd = jnp.concatenate(rows, axis=0)
    r = lax.broadcasted_iota(jnp.int32, qbd.shape, 0) % n_heads
    c = lax.broadcasted_iota(jnp.int32, qbd.shape, 1) // hd
    qbd = jnp.where(r == c, qbd, 0.0)
    gate = lax.dot_general(qbd, km, (((1,), (1,)), ((), ())),
                           precision=lax.Precision.HIGHEST, preferred_element_type=F32)
    blk = lax.broadcasted_iota(jnp.int32, gate.shape, 1).astype(F32)
    firsts = _topk_first_indices(gate, blk)
    lane = lax.broadcasted_iota(jnp.int32, o_ref.shape[1:], 1)
    out = jnp.zeros(o_ref.shape[1:], F32)
    for j, first in enumerate(firsts):
        out = jnp.where(lane == j, first, out)
    o_ref[0] = out.astype(jnp.int32)


def _gate_sample(zs3, kmean, n_heads, hd, q_col0):
    bsz, s, _ = zs3.shape
    nbp = kmean.shape[1]
    w = n_heads * hd
    assert nbp >= MOBA_TOP_K and n_heads % 8 == 0 and q_col0 % w == 0
    return pl.pallas_call(
        functools.partial(_gate_sample_kernel, n_heads=n_heads, hd=hd),
        grid=(bsz,),
        in_specs=[
            pl.BlockSpec((1, s, w), lambda b: (b, 0, q_col0 // w)),
            pl.BlockSpec((1, nbp, 1, w), lambda b: (b, 0, 0, 0)),
        ],
        out_specs=pl.BlockSpec((1, s * n_heads, 128), lambda b: (b, 0, 0)),
        out_shape=jax.ShapeDtypeStruct((bsz, s * n_heads, 128), jnp.int32),
        compiler_params=_params(("parallel",), 8 * (s * n_heads + nbp * 8) * w * 4),
        name="gate_sample",
    )(zs3, kmean)


def _moba_sample_kernel(pt_ref, sel_ref, q_ref, kn_ref, vn_ref, ga_ref, ck_ref, cv_ref, o_ref,
                        kbuf, vbuf, sem, *, layer, scale, n_heads):
    b = pl.program_id(0)
    h = pl.program_id(1)
    n_b = pl.num_programs(0)
    s_len, hd = q_ref.shape[1:]
    page = ck_ref.shape[2]
    ppb = MOBA_BLOCK // page
    step = b * n_heads + h
    n_steps = n_b * n_heads

    def copies(bb, hh, slot):
        out = []
        for t in range(s_len):
            for j in range(MOBA_TOP_K):
                blk = sel_ref[((bb * s_len + t) * n_heads + hh) * MOBA_TOP_K + j]
                for r in range(ppb):
                    pg = pt_ref[bb, blk * ppb + r]
                    dst = pl.ds(((t * MOBA_TOP_K + j) * ppb + r) * page, page)
                    out.append(pltpu.make_async_copy(ck_ref.at[layer, pg, :, hh, :], kbuf.at[slot, dst, :],
                                                     sem.at[slot, 0]))
                    out.append(pltpu.make_async_copy(cv_ref.at[layer, pg, :, hh, :], vbuf.at[slot, dst, :],
                                                     sem.at[slot, 1]))
        return out

    @pl.when(step == 0)
    def _():
        for cp in copies(b, h, 0):
            cp.start()

    @pl.when(step + 1 < n_steps)
    def _():
        nxt = step + 1
        for cp in copies(nxt // n_heads, nxt % n_heads, nxt % 2):
            cp.start()

    slot = step % 2
    for cp in copies(b, h, slot):
        cp.wait()

    q = q_ref[0] * (scale * LOG2E)
    kk = kbuf[slot]
    vv = vbuf[slot]
    s = lax.dot_general(q, kk, (((1,), (1,)), ((), ())), preferred_element_type=F32)
    owner = lax.broadcasted_iota(jnp.int32, s.shape, 1) // (MOBA_TOP_K * MOBA_BLOCK)
    s = jnp.where(owner == lax.broadcasted_iota(jnp.int32, s.shape, 0), s, NEG_BIG)
    s_new = lax.dot_general(q, kn_ref[0], (((1,), (1,)), ((), ())), preferred_element_type=F32)
    causal = lax.broadcasted_iota(jnp.int32, s_new.shape, 1) <= lax.broadcasted_iota(jnp.int32, s_new.shape, 0)
    s_new = jnp.where(causal, s_new, NEG_BIG)
    m = jnp.maximum(jnp.max(s, axis=-1, keepdims=True), jnp.max(s_new, axis=-1, keepdims=True))
    p = jnp.exp2(s - m)
    p_new = jnp.exp2(s_new - m)
    l = jnp.sum(p, axis=-1, keepdims=True) + jnp.sum(p_new, axis=-1, keepdims=True)
    acc = jnp.dot(p, vv, preferred_element_type=F32) + jnp.dot(p_new, vn_ref[0], preferred_element_type=F32)
    o_ref[0] = (acc / l) * _silu(ga_ref[0])


def _moba_sample(zs3, sel_flat, page_table, cache_k, cache_v, layer, n_heads, hd, q_col0, k_col0, v_col0, ga_col0):
    bsz, s_len, _ = zs3.shape
    page = cache_k.shape[2]
    assert MOBA_BLOCK % page == 0 and (page_table.shape[1] * page) % MOBA_BLOCK == 0 and s_len <= MOBA_BLOCK
    rows = s_len * MOBA_TOP_K * MOBA_BLOCK
    qc, kc, vc, gc = (c // hd for c in (q_col0, k_col0, v_col0, ga_col0))
    tok = lambda c0: pl.BlockSpec((1, s_len, hd), lambda b, h, pt, sel: (b, 0, c0 + h))
    return pl.pallas_call(
        functools.partial(_moba_sample_kernel, layer=layer, scale=hd ** -0.5, n_heads=n_heads),
        grid_spec=pltpu.PrefetchScalarGridSpec(
            num_scalar_prefetch=2,
            grid=(bsz, n_heads),
            in_specs=[tok(qc), tok(kc), tok(vc), tok(gc),
                      pl.BlockSpec(memory_space=pl.ANY), pl.BlockSpec(memory_space=pl.ANY)],
            out_specs=pl.BlockSpec((1, s_len, hd), lambda b, h, pt, sel: (b, 0, h)),
            scratch_shapes=[
                pltpu.VMEM((2, rows, hd), F32),
                pltpu.VMEM((2, rows, hd), F32),
                pltpu.SemaphoreType.DMA((2, 2)),
            ],
        ),
        out_shape=jax.ShapeDtypeStruct((bsz, s_len, n_heads * hd), F32),
        compiler_params=_params(("arbitrary", "arbitrary"), 4 * rows * hd * 4 + 8 * rows * 8 * 4),
        name="moba_sample",
    )(page_table, sel_flat, zs3, zs3, zs3, zs3, cache_k, cache_v)


def _conv_sample_kernel(cv_ref, cg_ref, st_ref, w_ref, b_ref, lg_ref, lb_ref, u_ref, nst_ref, xp, ybuf):
    s_len = cv_ref.shape[1]
    n_ctx = st_ref.shape[1]
    cw = w_ref.shape[0]
    xp[pl.ds(0, n_ctx), :] = st_ref[0]
    xp[pl.ds(n_ctx, s_len), :] = cv_ref[0] * jax.nn.sigmoid(cg_ref[0])
    w = w_ref[...]
    for t in range(s_len):
        ybuf[pl.ds(t, 1), :] = jnp.sum(w * xp[pl.ds(t, cw), :], axis=0, keepdims=True)
    y = ybuf[pl.ds(0, s_len), :] + b_ref[...]
    mu = jnp.mean(y, axis=-1, keepdims=True)
    yc = y - mu
    var = jnp.mean(yc * yc, axis=-1, keepdims=True)
    yn = yc * lax.rsqrt(var + LN_EPS) * lg_ref[...] + lb_ref[...]
    u_ref[0] = _silu(yn)
    nst_ref[0] = xp[pl.ds(s_len, n_ctx), :]


def _conv_sample(zs3, cv_col0, cg_col0, state, w_dw, b_dw, ln_g, ln_b):
    bsz, s_len, _ = zs3.shape
    n_ctx, c = state.shape[1:]
    cw = w_dw.shape[0]
    assert n_ctx == cw - 1
    rows = -(-(n_ctx + s_len) // 8) * 8
    return pl.pallas_call(
        _conv_sample_kernel,
        grid=(bsz,),
        in_specs=[
            pl.BlockSpec((1, s_len, c), lambda b: (b, 0, cv_col0 // c)),
            pl.BlockSpec((1, s_len, c), lambda b: (b, 0, cg_col0 // c)),
            pl.BlockSpec((1, n_ctx, c), lambda b: (b, 0, 0)),
            pl.BlockSpec((cw, c), lambda b: (0, 0)),
            pl.BlockSpec((1, c), lambda b: (0, 0)),
            pl.BlockSpec((1, c), lambda b: (0, 0)),
            pl.BlockSpec((1, c), lambda b: (0, 0)),
        ],
        out_specs=[pl.BlockSpec((1, s_len, c), lambda b: (b, 0, 0)), pl.BlockSpec((1, n_ctx, c), lambda b: (b, 0, 0))],
        out_shape=[jax.ShapeDtypeStruct((bsz, s_len, c), F32), jax.ShapeDtypeStruct((bsz, n_ctx, c), F32)],
        scratch_shapes=[pltpu.VMEM((rows, c), F32), pltpu.VMEM((8, c), F32)],
        compiler_params=_params(("parallel",), 16 * rows * c * 4),
        name="conv_sample",
    )(zs3, zs3, state, w_dw, b_dw.reshape(1, c), ln_g.reshape(1, c), ln_b.reshape(1, c))


def kernel(x_prompt, x_sample, cache_k, cache_v, state_conv, page_table, norm_in_g, w_in, w_dw, b_dw, ln_g, ln_b,
           w_pw2, w_out, norm_f_g):
    bp, t, d = x_prompt.shape
    bs, s_len, _ = x_sample.shape
    depth = w_in.shape[0]
    n_heads, hd = cache_k.shape[3:]
    att_w = n_heads * hd
    c = state_conv.shape[-1]
    n_ctx = state_conv.shape[2]
    assert att_w == c, "column-block addressing of the combined projection assumes equal group widths"
    q0, k0, v0, ga0 = 0, att_w, 2 * att_w, 3 * att_w
    cv0, cg0, gc0 = 4 * att_w, 4 * att_w + c, 4 * att_w + 2 * c

    xp = x_prompt.reshape(bp * t, d)
    xs = x_sample.reshape(bs * s_len, d)
    outs = {name: [] for name in ("kp", "vp", "cp", "ks", "vs", "cs")}
    for l in range(depth):
        w_in_b = w_in[l].astype(BF16)
        w_pw2_b = w_pw2[l].astype(BF16)
        w_out_b = w_out[l].astype(BF16)
        w_out_att, w_out_conv = w_out_b[:att_w], w_out_b[att_w:]

        hp = _rms_norm(xp, norm_in_g[l], BF16)
        zp, kmean_s = _matmul_kmean(hp, w_in_b, cache_k, l, page_table)
        z_att, z_conv, kp, vp, cp = [], [], [], [], []
        for bi in range(bp):
            zb = zp[bi * t:(bi + 1) * t] if bp > 1 else zp
            kmean = _kmean_prompt(zb, k0, att_w)
            sel_p = _gate_prompt(zb, kmean, n_heads, hd, q0)
            z_att.append(_moba_prompt(zb, sel_p, n_heads, hd, q0, k0, v0, ga0))
            u, tail = _conv_prompt(zb, cv0, cg0, c, jnp.zeros((n_ctx, c), F32), w_dw[l], b_dw[l], ln_g[l], ln_b[l])
            z_conv.append(_matmul(u, w_pw2_b, 0, c, BF16, gate=zb, gate_col0=gc0))
            kp.append(zb[:, k0:k0 + att_w].reshape(t, n_heads, hd))
            vp.append(zb[:, v0:v0 + att_w].reshape(t, n_heads, hd))
            cp.append(tail)
        z_att = z_att[0] if bp == 1 else jnp.concatenate(z_att, axis=0)
        z_conv = z_conv[0] if bp == 1 else jnp.concatenate(z_conv, axis=0)
        xp = _outproj(z_att, z_conv, w_out_att, w_out_conv, xp)
        outs["kp"].append(jnp.stack(kp)); outs["vp"].append(jnp.stack(vp)); outs["cp"].append(jnp.stack(cp))

        hs = _rms_norm(xs, norm_in_g[l], BF16)
        zs = _matmul(hs, w_in_b, 0, w_in_b.shape[1], F32)
        zs3 = zs.reshape(bs, s_len, zs.shape[1])
        sel = _gate_sample(zs3, kmean_s, n_heads, hd, q0)[:, :, :MOBA_TOP_K].reshape(-1)
        z_att_s = _moba_sample(zs3, sel, page_table, cache_k, cache_v, l, n_heads, hd, q0, k0, v0, ga0)
        u_s, new_state = _conv_sample(zs3, cv0, cg0, state_conv[l], w_dw[l], b_dw[l], ln_g[l], ln_b[l])
        z_conv_s = _matmul(u_s.reshape(bs * s_len, c), w_pw2_b, 0, c, BF16, gate=zs, gate_col0=gc0)
        xs = _outproj(z_att_s.reshape(bs * s_len, att_w), z_conv_s, w_out_att, w_out_conv, xs)
        outs["ks"].append(zs[:, k0:k0 + att_w].reshape(bs, s_len, n_heads, hd))
        outs["vs"].append(zs[:, v0:v0 + att_w].reshape(bs, s_len, n_heads, hd))
        outs["cs"].append(new_state)

    y_prompt = _rms_norm(xp, norm_f_g, F32).reshape(bp, t, d)
    y_sample = _rms_norm(xs, norm_f_g, F32).reshape(bs, s_len, d)
    return (y_prompt, y_sample, jnp.stack(outs["kp"]), jnp.stack(outs["vp"]), jnp.stack(outs["cp"]),
            jnp.stack(outs["ks"]), jnp.stack(outs["vs"]), jnp.stack(outs["cs"]))
```

```python
import functools
import math

import jax
import jax.numpy as jnp
from jax import lax
from jax.experimental import pallas as pl
from jax.experimental.pallas import tpu as pltpu

MOBA_BLOCK = 256
MOBA_TOP_K = 3
RMS_EPS = 1e-6
LN_EPS = 1e-5
LOG2E = math.log2(math.e)
NEG_BIG = -1e30
POS_BIG = 1e30
CONV_HALO = 32
SUBLANES = 8
CONV_ROWS = 32
CONV_LANES = 256
ONES_ROWS = 16
SAMPLE_SLOTS = 3
PROMPT_UNROLL = 8
PROMPT_QBLOCKS = 2
V7X_SCOPED_VMEM_BYTES = 60000 * 1024
F32 = jnp.float32
BF16 = jnp.bfloat16


def _params(semantics, vmem_bytes):
    limit = int(min(V7X_SCOPED_VMEM_BYTES, max(vmem_bytes, 16 * 1024 * 1024)))
    return pltpu.CompilerParams(dimension_semantics=semantics, vmem_limit_bytes=limit)


def _tile(n, pref, unit):
    if n <= pref:
        return n
    t = (pref // unit) * unit
    while t >= unit:
        if n % t == 0:
            return t
        t -= unit
    raise ValueError(f"no tile for {n} (pref {pref}, unit {unit})")


def _silu(x):
    return x * jax.nn.sigmoid(x)


def _rms_kernel(x_ref, g_ref, o_ref):
    x = x_ref[...]
    ms = jnp.mean(x * x, axis=-1, keepdims=True)
    y = x * lax.rsqrt(ms + RMS_EPS)
    o_ref[...] = (y * g_ref[...]).astype(o_ref.dtype)


def _rms_norm(x, g, out_dtype):
    m, d = x.shape
    tm = _tile(m, 256, 8)
    blk = tm * d * 4
    return pl.pallas_call(
        _rms_kernel,
        grid=(m // tm,),
        in_specs=[pl.BlockSpec((tm, d), lambda i: (i, 0)), pl.BlockSpec((1, d), lambda i: (0, 0))],
        out_specs=pl.BlockSpec((tm, d), lambda i: (i, 0)),
        out_shape=jax.ShapeDtypeStruct((m, d), out_dtype),
        compiler_params=_params(("parallel",), 6 * blk),
        name="rms_norm",
    )(x, g.reshape(1, d))


def _mm_kernel(*refs, gated):
    if gated:
        a_ref, b_ref, g_ref, o_ref = refs
    else:
        a_ref, b_ref, o_ref = refs
    acc = jnp.dot(a_ref[...].astype(BF16), b_ref[...].astype(BF16), preferred_element_type=F32)
    if gated:
        acc = acc * _silu(g_ref[...])
    o_ref[...] = acc.astype(o_ref.dtype)


def _matmul(a, b, col0, n, out_dtype, gate=None, gate_col0=0):
    m, k = a.shape
    tm = _tile(m, 1024, 8)
    tn = _tile(n, 512, 128)
    assert col0 % tn == 0 and gate_col0 % tn == 0
    cb, gb = col0 // tn, gate_col0 // tn
    in_specs = [pl.BlockSpec((tm, k), lambda i, j: (i, 0)), pl.BlockSpec((k, tn), lambda i, j: (0, j + cb))]
    args = [a, b]
    if gate is not None:
        in_specs.append(pl.BlockSpec((tm, tn), lambda i, j: (i, j + gb)))
        args.append(gate)
    vmem = 2 * (tm * k * a.dtype.itemsize + k * tn * b.dtype.itemsize + 3 * tm * tn * 4) + (tm + tn) * k * 2
    return pl.pallas_call(
        functools.partial(_mm_kernel, gated=gate is not None),
        grid=(m // tm, n // tn),
        in_specs=in_specs,
        out_specs=pl.BlockSpec((tm, tn), lambda i, j: (i, j)),
        out_shape=jax.ShapeDtypeStruct((m, n), out_dtype),
        compiler_params=_params(("parallel", "parallel"), vmem),
        name="matmul",
    )(*args)


def _mm_kmean_kernel(pt_ref, a_ref, b_ref, ck_ref, o_ref, km_ref, pbuf, sem, *,
                     layer, n_blocks, seq_blocks, blocks_per_step, n_j):
    step = pl.program_id(0) * n_j + pl.program_id(1)
    ppb = pbuf.shape[1] // blocks_per_step
    n_heads, hd = pbuf.shape[-2:]
    n_active = -(-n_blocks // blocks_per_step)

    def copies(st, slot):
        out = []
        for r in range(blocks_per_step):
            g = jnp.minimum(st * blocks_per_step + r, n_blocks - 1)
            for p in range(ppb):
                pg = pt_ref[g // seq_blocks, (g % seq_blocks) * ppb + p]
                out.append(pltpu.make_async_copy(ck_ref.at[layer, pg], pbuf.at[slot, r * ppb + p], sem.at[slot]))
        return out

    @pl.when(step == 0)
    def _():
        for cp in copies(0, 0):
            cp.start()

    @pl.when(step + 1 < n_active)
    def _():
        for cp in copies(step + 1, (step + 1) % 2):
            cp.start()

    slot = step % 2

    @pl.when(step < n_active)
    def _():
        for cp in copies(step, slot):
            cp.wait()

    means = []
    for r in range(blocks_per_step):
        tot = jnp.sum(pbuf[slot, r * ppb], axis=0)
        for p in range(1, ppb):
            tot = tot + jnp.sum(pbuf[slot, r * ppb + p], axis=0)
        means.append(tot * (1.0 / MOBA_BLOCK))
    o_ref[...] = jnp.dot(a_ref[...], b_ref[...], preferred_element_type=F32)

    @pl.when(step < n_active)
    def _():
        for r in range(blocks_per_step):
            for h in range(n_heads):
                km_ref[r, :, pl.ds(h * hd, hd)] = means[r][h:h + 1, :]


def _matmul_kmean(a, b, cache_k, layer, page_table):
    m, k = a.shape
    n = b.shape[1]
    _, _, page, n_heads, hd = cache_k.shape
    bsz, n_pages = page_table.shape
    ppb = MOBA_BLOCK // page
    seq_blocks = n_pages // ppb
    n_blocks = bsz * seq_blocks
    tm = _tile(m, 1024, 8)
    tn = _tile(n, 512, 128)
    n_i, n_j = m // tm, n // tn
    blocks_per_step = -(-n_blocks // (n_i * n_j))
    n_active = -(-n_blocks // blocks_per_step)
    pages = blocks_per_step * ppb
    vmem = 2 * (tm * k * 2 + k * tn * 2 + 2 * tm * tn * 4) + 2 * pages * page * n_heads * hd * 4
    z, km = pl.pallas_call(
        functools.partial(_mm_kmean_kernel, layer=layer, n_blocks=n_blocks, seq_blocks=seq_blocks,
                          blocks_per_step=blocks_per_step, n_j=n_j),
        grid_spec=pltpu.PrefetchScalarGridSpec(
            num_scalar_prefetch=1,
            grid=(n_i, n_j),
            in_specs=[
                pl.BlockSpec((tm, k), lambda i, j, pt: (i, 0)),
                pl.BlockSpec((k, tn), lambda i, j, pt: (0, j)),
                pl.BlockSpec(memory_space=pl.ANY),
            ],
            out_specs=[
                pl.BlockSpec((tm, tn), lambda i, j, pt: (i, j)),
                pl.BlockSpec((blocks_per_step, 1, n_heads * hd),
                             lambda i, j, pt: (jnp.minimum(i * n_j + j, n_active - 1), 0, 0)),
            ],
            scratch_shapes=[
                pltpu.VMEM((2, pages, page, n_heads, hd), F32),
                pltpu.SemaphoreType.DMA((2,)),
            ],
        ),
        out_shape=[
            jax.ShapeDtypeStruct((m, n), F32),
            jax.ShapeDtypeStruct((n_active * blocks_per_step, 1, n_heads * hd), F32),
        ],
        compiler_params=_params(("arbitrary", "arbitrary"), vmem),
        name="matmul_kmean",
    )(page_table, a, b, cache_k)
    return z, km[:n_blocks].reshape(bsz, seq_blocks, 1, n_heads * hd)


def _outproj_kernel(a1_ref, a2_ref, w1_ref, w2_ref, x_ref, o_ref):
    acc = jnp.dot(a1_ref[...].astype(BF16), w1_ref[...].astype(BF16), preferred_element_type=F32)
    acc += jnp.dot(a2_ref[...].astype(BF16), w2_ref[...].astype(BF16), preferred_element_type=F32)
    o_ref[...] = x_ref[...] + acc


def _outproj(a1, a2, w, x):
    m, k = a1.shape
    n = x.shape[1]
    assert a2.shape[1] == k and w.shape[0] == 2 * k
    tm = _tile(m, 1024, 8)
    tn = _tile(n, 512, 128)
    vmem = (2 * (tm * k * (a1.dtype.itemsize + a2.dtype.itemsize) + 2 * k * tn * w.dtype.itemsize + 3 * tm * tn * 4)
            + 2 * (tm + tn) * k * 2)
    return pl.pallas_call(
        _outproj_kernel,
        grid=(m // tm, n // tn),
        in_specs=[
            pl.BlockSpec((tm, k), lambda i, j: (i, 0)),
            pl.BlockSpec((tm, k), lambda i, j: (i, 0)),
            pl.BlockSpec((k, tn), lambda i, j: (0, j)),
            pl.BlockSpec((k, tn), lambda i, j: (1, j)),
            pl.BlockSpec((tm, tn), lambda i, j: (i, j)),
        ],
        out_specs=pl.BlockSpec((tm, tn), lambda i, j: (i, j)),
        out_shape=jax.ShapeDtypeStruct((m, n), F32),
        compiler_params=_params(("parallel", "parallel"), vmem),
        name="outproj",
    )(a1, a2, w, w, x)


def _topk_first_indices(gate, blk, axis=-1):
    nb = gate.shape[axis]
    firsts = []
    for _ in range(MOBA_TOP_K):
        mx = jnp.max(gate, axis=axis, keepdims=True)
        first = jnp.min(jnp.where(gate == mx, blk, float(nb)), axis=axis, keepdims=True)
        firsts.append(first)
        gate = jnp.where(blk == first, -jnp.inf, gate)
    return firsts


def _kmean_prompt_kernel(k_ref, o_ref):
    for r in range(o_ref.shape[0]):
        o_ref[r] = jnp.sum(k_ref[pl.ds(r * MOBA_BLOCK, MOBA_BLOCK), :], axis=0, keepdims=True) * (1.0 / MOBA_BLOCK)


def _kmean_prompt(z, col0, w):
    t = z.shape[0]
    nb = t // MOBA_BLOCK
    per_step = _tile(nb, 4, 1)
    assert col0 % w == 0
    return pl.pallas_call(
        _kmean_prompt_kernel,
        grid=(nb // per_step,),
        in_specs=[pl.BlockSpec((per_step * MOBA_BLOCK, w), lambda i: (i, col0 // w))],
        out_specs=pl.BlockSpec((per_step, 1, w), lambda i: (i, 0, 0)),
        out_shape=jax.ShapeDtypeStruct((nb, 1, w), F32),
        compiler_params=_params(("parallel",), 3 * per_step * MOBA_BLOCK * w * 4),
        name="kmean_prompt",
    )(z)


def _gate_prompt_kernel(q_ref, km_ref, o_ref):
    j = pl.program_id(1)
    tq = q_ref.shape[0]
    nb = km_ref.shape[0]
    gate = lax.dot_general(km_ref[:, 0, :], q_ref[...], (((1,), (1,)), ((), ())),
                           precision=lax.Precision.HIGHEST, preferred_element_type=F32)
    blk = lax.broadcasted_iota(jnp.int32, (nb, tq), 0)
    own = (lax.broadcasted_iota(jnp.int32, (nb, tq), 1) + j * tq) // MOBA_BLOCK
    past = blk < own
    blk_f = blk.astype(F32)
    firsts = _topk_first_indices(jnp.where(past, gate, -jnp.inf), blk_f, axis=0)
    sel = jnp.zeros((nb, tq), F32)
    for first in firsts:
        sel = jnp.maximum(sel, jnp.where(blk_f == first, 1.0, 0.0))
    sel = jnp.where(past, sel, 0.0)
    o_ref[0] = jnp.where(blk == own, 1.0, sel)


def _gate_prompt(z, kmean, n_heads, hd, q_col0):
    t = z.shape[0]
    nb = kmean.shape[0]
    tq = _tile(t, 2048, MOBA_BLOCK)
    return pl.pallas_call(
        _gate_prompt_kernel,
        grid=(n_heads, t // tq),
        in_specs=[
            pl.BlockSpec((tq, hd), lambda h, j: (j, q_col0 // hd + h)),
            pl.BlockSpec((nb, 1, hd), lambda h, j: (0, 0, h)),
        ],
        out_specs=pl.BlockSpec((1, nb, tq), lambda h, j: (h, 0, j)),
        out_shape=jax.ShapeDtypeStruct((n_heads, nb, t), F32),
        compiler_params=_params(("parallel", "parallel"), 4 * tq * hd * 4 + 24 * nb * tq * 4),
        name="gate_prompt",
    )(z, kmean)


def _moba_prompt_kernel(q_ref, qn_ref, k_ref, v_ref, sel_ref, ga_ref, o_ref, kb_sc, vt_sc, bias_sc, s_sc, bm_sc,
                        acc_sc, m_sc, slot_sc, *, scale):
    i = pl.program_id(1)
    bq, hd = q_ref.shape
    qblocks = bq // MOBA_BLOCK
    n_chunk_total = vt_sc.shape[0]
    nt = (((1,), (1,)), ((), ()))
    chunk = PROMPT_UNROLL * MOBA_BLOCK

    @pl.when(i == 0)
    def _():
        kb_sc[...] = k_ref[...].astype(BF16)
        key = lax.broadcasted_iota(jnp.int32, (MOBA_BLOCK, bq), 0)
        qry = lax.broadcasted_iota(jnp.int32, (MOBA_BLOCK, bq), 1)
        bias_sc[0] = jnp.zeros((MOBA_BLOCK, bq), F32)
        for r in range(qblocks):
            bias_sc[1 + r] = jnp.where(qry - r * MOBA_BLOCK >= key, 0.0, NEG_BIG)

        def stage(c, carry):
            for u in range(PROMPT_UNROLL):
                off = pl.multiple_of((c * PROMPT_UNROLL + u) * MOBA_BLOCK, MOBA_BLOCK)
                vt_sc[c, pl.ds(0, hd), pl.ds(u * MOBA_BLOCK, MOBA_BLOCK)] = (
                    v_ref[pl.ds(off, MOBA_BLOCK), :].T.astype(BF16))
            vt_sc[c, pl.ds(hd, ONES_ROWS), :] = jnp.ones((ONES_ROWS, chunk), BF16)
            return carry

        lax.fori_loop(0, n_chunk_total, stage, 0)

    qb = (q_ref[...] * (scale * LOG2E)).astype(BF16)
    qb_next = (qn_ref[...] * (scale * LOG2E)).astype(BF16)
    first_own = i * qblocks
    first_own_next = jnp.minimum(i + 1, pl.num_programs(1) - 1) * qblocks
    n_chunks = (first_own + qblocks - 1) // PROMPT_UNROLL + 1

    def scores_into(slot, q_bf16, tile_first_own, c):
        off = pl.multiple_of(c * chunk, chunk)
        s = lax.dot_general(kb_sc[pl.ds(off, chunk), :], q_bf16, nt, preferred_element_type=F32)
        for u in range(PROMPT_UNROLL):
            r = c * PROMPT_UNROLL + u - tile_first_own
            which = jnp.where((r >= 0) & (r < qblocks), r + 1, 0)
            su = s[u * MOBA_BLOCK:(u + 1) * MOBA_BLOCK] + bias_sc[which]
            s_sc[slot, pl.ds(u * MOBA_BLOCK, MOBA_BLOCK), :] = su
            bm_sc[slot, pl.ds(u, 1), :] = jnp.max(su, axis=0, keepdims=True)

    @pl.when(i == 0)
    def _():
        slot_sc[0] = 0
        scores_into(0, qb, first_own, 0)

    slot0 = slot_sc[0]
    acc_sc[...] = jnp.zeros(acc_sc.shape, F32)
    m_sc[...] = jnp.full(m_sc.shape, NEG_BIG, F32)

    def trip(c, carry):
        cur = (slot0 + c) % 2
        m_old = m_sc[...]
        picked = sel_ref[0, pl.ds(pl.multiple_of(c * PROMPT_UNROLL, PROMPT_UNROLL), PROMPT_UNROLL), :] > 0.5
        m_new = jnp.maximum(m_old, jnp.max(jnp.where(picked, bm_sc[cur], NEG_BIG), axis=0, keepdims=True))
        ps = []
        for u in range(PROMPT_UNROLL):
            m_eff = jnp.where(picked[u:u + 1], m_new, POS_BIG)
            ps.append(jnp.exp2(s_sc[cur, pl.ds(u * MOBA_BLOCK, MOBA_BLOCK), :] - m_eff).astype(BF16))
        pv = jnp.dot(vt_sc[c], jnp.concatenate(ps, axis=0), preferred_element_type=F32)
        acc_sc[...] = acc_sc[...] * jnp.exp2(m_old - m_new) + pv
        m_sc[...] = m_new
        last = c + 1 == n_chunks
        scores_into(1 - cur, jnp.where(last, qb_next, qb), jnp.where(last, first_own_next, first_own),
                    jnp.where(last, 0, c + 1))
        return carry

    lax.fori_loop(0, n_chunks, trip, 0)
    slot_sc[0] = (slot0 + n_chunks) % 2

    acc = acc_sc[...]
    out = (acc[:hd] / acc[hd:hd + 1]).T
    o_ref[...] = (out * _silu(ga_ref[...])).astype(o_ref.dtype)


def _moba_prompt(z, sel, n_heads, hd, q_col0, k_col0, v_col0, ga_col0):
    t = z.shape[0]
    chunk = PROMPT_UNROLL * MOBA_BLOCK
    bq = PROMPT_QBLOCKS * MOBA_BLOCK
    assert t % chunk == 0 and t % bq == 0
    nb = t // MOBA_BLOCK
    qc, kc, vc, gc = (c // hd for c in (q_col0, k_col0, v_col0, ga_col0))
    scratch = [
        pltpu.VMEM((t, hd), BF16),
        pltpu.VMEM((t // chunk, hd + ONES_ROWS, chunk), BF16),
        pltpu.VMEM((1 + PROMPT_QBLOCKS, MOBA_BLOCK, bq), F32),
        pltpu.VMEM((2, chunk, bq), F32),
        pltpu.VMEM((2, PROMPT_UNROLL, bq), F32),
        pltpu.VMEM((hd + ONES_ROWS, bq), F32),
        pltpu.VMEM((1, bq), F32),
        pltpu.SMEM((1,), jnp.int32),
    ]
    n_tiles = t // bq
    vmem = (2 * (2 * t * hd * 4 + 4 * bq * hd * 4 + nb * bq * 4)
            + t * hd * 2 + (t // chunk) * (hd + ONES_ROWS) * chunk * 2 + (1 + PROMPT_QBLOCKS) * MOBA_BLOCK * bq * 4
            + 5 * chunk * bq * 4)
    return pl.pallas_call(
        functools.partial(_moba_prompt_kernel, scale=hd ** -0.5),
        grid=(n_heads, n_tiles),
        in_specs=[
            pl.BlockSpec((bq, hd), lambda h, i: (i, qc + h)),
            pl.BlockSpec((bq, hd), lambda h, i: (jnp.minimum(i + 1, n_tiles - 1), qc + h)),
            pl.BlockSpec((t, hd), lambda h, i: (0, kc + h)),
            pl.BlockSpec((t, hd), lambda h, i: (0, vc + h)),
            pl.BlockSpec((1, nb, bq), lambda h, i: (h, 0, i)),
            pl.BlockSpec((bq, hd), lambda h, i: (i, gc + h)),
        ],
        out_specs=pl.BlockSpec((bq, hd), lambda h, i: (i, h)),
        out_shape=jax.ShapeDtypeStruct((t, n_heads * hd), BF16),
        scratch_shapes=scratch,
        compiler_params=_params(("parallel", "arbitrary"), vmem),
        name="moba_prompt",
    )(z, z, z, z, sel, z)


def _conv_prompt_kernel(cv_ref, cvp_ref, cg_ref, cgp_ref, ctx_ref, w_ref, b_ref, lg_ref, lb_ref,
                        u_ref, tail_ref, gbuf, shifted, ybuf, wb):
    i = pl.program_id(0)
    tt = cv_ref.shape[0]
    cw = w_ref.shape[0]
    glu = cv_ref[...] * jax.nn.sigmoid(cg_ref[...])
    gbuf[pl.ds(CONV_HALO, tt), :] = glu

    @pl.when(i == 0)
    def _():
        gbuf[pl.ds(0, CONV_HALO), :] = ctx_ref[...]

    @pl.when(i > 0)
    def _():
        gbuf[pl.ds(0, CONV_HALO), :] = cvp_ref[...] * jax.nn.sigmoid(cgp_ref[...])

    span = shifted.shape[1]
    for s in range(1, SUBLANES):
        shifted[s - 1] = gbuf[pl.ds(s, span), :]
    base = CONV_HALO - (cw - 1)

    @pl.when(i == 0)
    def _():
        for k in range(cw):
            wb[k] = jnp.broadcast_to(w_ref[k:k + 1, :], wb.shape[1:])

    def rows_chunk(j, carry):
        r0 = pl.multiple_of(j * CONV_ROWS, CONV_ROWS)
        for c0 in range(0, ybuf.shape[1], CONV_LANES):
            lanes = pl.ds(c0, CONV_LANES)
            acc = None
            for s in range(SUBLANES):
                wholes = [w for w in range((base + cw - 1) // SUBLANES + 1) if 0 <= w * SUBLANES + s - base < cw]
                rows = pl.ds(r0 + wholes[0] * SUBLANES, CONV_ROWS + (wholes[-1] - wholes[0]) * SUBLANES)
                xs = gbuf[rows, lanes] if s == 0 else shifted[s - 1, rows, lanes]
                for w in wholes:
                    k = w * SUBLANES + s - base
                    x = xs[(w - wholes[0]) * SUBLANES:(w - wholes[0]) * SUBLANES + CONV_ROWS]
                    wk = jnp.concatenate([wb[k, :, lanes]] * (CONV_ROWS // SUBLANES), axis=0)
                    acc = wk * x if acc is None else acc + wk * x
            ybuf[pl.ds(r0, CONV_ROWS), lanes] = acc
        return carry

    lax.fori_loop(0, tt // CONV_ROWS, rows_chunk, 0)
    y = ybuf[...] + b_ref[...]
    mu = jnp.mean(y, axis=-1, keepdims=True)
    yc = y - mu
    var = jnp.mean(yc * yc, axis=-1, keepdims=True)
    yn = yc * lax.rsqrt(var + LN_EPS) * lg_ref[...] + lb_ref[...]
    u_ref[...] = _silu(yn).astype(u_ref.dtype)
    tail_ref[...] = gbuf[pl.ds(tt, CONV_HALO), :]


def _conv_prompt(z, cv_col0, cg_col0, c, ctx, w_dw, b_dw, ln_g, ln_b):
    t = z.shape[0]
    cw = w_dw.shape[0]
    n_ctx = cw - 1
    assert n_ctx <= CONV_HALO
    tt = _tile(t, 256, CONV_HALO)
    assert t % tt == 0 and tt % CONV_HALO == 0
    r = tt // CONV_HALO
    cvb, cgb = cv_col0 // c, cg_col0 // c
    ctx_pad = jnp.concatenate([jnp.zeros((CONV_HALO - n_ctx, c), F32), ctx], axis=0)
    prev = lambda i: jnp.maximum(i * r - 1, 0)
    assert tt % CONV_ROWS == 0 and CONV_ROWS % SUBLANES == 0 and CONV_HALO % SUBLANES == 0
    span = tt + CONV_HALO - SUBLANES
    vmem = (2 * (2 * tt * c * 4 + 3 * CONV_HALO * c * 4 + tt * c * 2 + (cw + 3) * c * 4)
            + ((tt + CONV_HALO) + (SUBLANES - 1) * span + 5 * tt) * c * 4)
    u, tail = pl.pallas_call(
        _conv_prompt_kernel,
        grid=(t // tt,),
        in_specs=[
            pl.BlockSpec((tt, c), lambda i: (i, cvb)),
            pl.BlockSpec((CONV_HALO, c), lambda i: (prev(i), cvb)),
            pl.BlockSpec((tt, c), lambda i: (i, cgb)),
            pl.BlockSpec((CONV_HALO, c), lambda i: (prev(i), cgb)),
            pl.BlockSpec((CONV_HALO, c), lambda i: (0, 0)),
            pl.BlockSpec((cw, c), lambda i: (0, 0)),
            pl.BlockSpec((1, c), lambda i: (0, 0)),
            pl.BlockSpec((1, c), lambda i: (0, 0)),
            pl.BlockSpec((1, c), lambda i: (0, 0)),
        ],
        out_specs=[pl.BlockSpec((tt, c), lambda i: (i, 0)), pl.BlockSpec((CONV_HALO, c), lambda i: (0, 0))],
        out_shape=[jax.ShapeDtypeStruct((t, c), BF16), jax.ShapeDtypeStruct((CONV_HALO, c), F32)],
        scratch_shapes=[
            pltpu.VMEM((CONV_HALO + tt, c), F32),
            pltpu.VMEM((SUBLANES - 1, span, c), F32),
            pltpu.VMEM((tt, c), F32),
            pltpu.VMEM((cw, SUBLANES, c), F32),
        ],
        compiler_params=_params(("arbitrary",), vmem),
        name="conv_prompt",
    )(z, z, z, z, ctx_pad, w_dw, b_dw.reshape(1, c), ln_g.reshape(1, c), ln_b.reshape(1, c))
    return u, tail[CONV_HALO - n_ctx:]


def _gate_sample_kernel(q_ref, km_ref, o_ref, *, n_heads, hd):
    s = q_ref.shape[1]
    km = km_ref[0, :, 0, :]
    nbp, w = km.shape
    rows = [jnp.broadcast_to(q_ref[0, t:t + 1, :], (n_heads, w)) for t in range(s)]
    qbd = jnp.concatenate(rows, axis=0)
    r = lax.broadcasted_iota(jnp.int32, qbd.shape, 0) % n_heads
    c = lax.broadcasted_iota(jnp.int32, qbd.shape, 1) // hd
    qbd = jnp.where(r == c, qbd, 0.0)
    gate = lax.dot_general(qbd, km, (((1,), (1,)), ((), ())),
                           precision=lax.Precision.HIGHEST, preferred_element_type=F32)
    blk = lax.broadcasted_iota(jnp.int32, gate.shape, 1).astype(F32)
    firsts = _topk_first_indices(gate, blk)
    lane = lax.broadcasted_iota(jnp.int32, o_ref.shape[1:], 1)
    out = jnp.zeros(o_ref.shape[1:], F32)
    for j, first in enumerate(firsts):
        out = jnp.where(lane == j, first, out)
    o_ref[0] = out.astype(jnp.int32)


def _gate_sample(zs3, kmean, n_heads, hd, q_col0):
    bsz, s, _ = zs3.shape
    nbp = kmean.shape[1]
    w = n_heads * hd
    assert nbp >= MOBA_TOP_K and n_heads % 8 == 0 and q_col0 % w == 0
    return pl.pallas_call(
        functools.partial(_gate_sample_kernel, n_heads=n_heads, hd=hd),
        grid=(bsz,),
        in_specs=[
            pl.BlockSpec((1, s, w), lambda b: (b, 0, q_col0 // w)),
            pl.BlockSpec((1, nbp, 1, w), lambda b: (b, 0, 0, 0)),
        ],
        out_specs=pl.BlockSpec((1, s * n_heads, 128), lambda b: (b, 0, 0)),
        out_shape=jax.ShapeDtypeStruct((bsz, s * n_heads, 128), jnp.int32),
        compiler_params=_params(("parallel",), 8 * (s * n_heads + nbp * 8) * w * 4),
        name="gate_sample",
    )(zs3, kmean)


def _moba_sample_kernel(pt_ref, sel_ref, q_ref, kn_ref, vn_ref, ga_ref, ck_ref, cv_ref, o_ref,
                        kbuf, vbuf, sem, *, layer, scale, n_heads):
    b = pl.program_id(0)
    h = pl.program_id(1)
    n_b = pl.num_programs(0)
    s_len, hd = q_ref.shape[1:]
    page = ck_ref.shape[2]
    ppb = MOBA_BLOCK // page
    step = b * n_heads + h
    n_steps = n_b * n_heads

    def copies(bb, hh, slot):
        out = []
        for t in range(s_len):
            for j in range(MOBA_TOP_K):
                blk = sel_ref[((bb * s_len + t) * n_heads + hh) * MOBA_TOP_K + j]
                for r in range(ppb):
                    pg = pt_ref[bb, blk * ppb + r]
                    dst = pl.ds(((t * MOBA_TOP_K + j) * ppb + r) * page, page)
                    out.append(pltpu.make_async_copy(ck_ref.at[layer, pg, :, hh, :], kbuf.at[slot, dst, :],
                                                     sem.at[slot, 0]))
                    out.append(pltpu.make_async_copy(cv_ref.at[layer, pg, :, hh, :], vbuf.at[slot, dst, :],
                                                     sem.at[slot, 1]))
        return out

    def start_all(cps):
        for n, cp in enumerate(cps):
            cp.start(priority=n % 2)

    n_slots = kbuf.shape[0]
    ahead = n_slots - 1

    @pl.when(step == 0)
    def _():
        for st in range(ahead):
            if st < n_heads:
                start_all(copies(0, st, st))

    @pl.when(step + ahead < n_steps)
    def _():
        nxt = step + ahead
        start_all(copies(nxt // n_heads, nxt % n_heads, nxt % n_slots))

    slot = step % n_slots
    for cp in copies(b, h, slot):
        cp.wait()

    q = q_ref[0] * (scale * LOG2E)
    kk = kbuf[slot]
    vv = vbuf[slot]
    s = lax.dot_general(q, kk, (((1,), (1,)), ((), ())), preferred_element_type=F32)
    owner = lax.broadcasted_iota(jnp.int32, s.shape, 1) // (MOBA_TOP_K * MOBA_BLOCK)
    s = jnp.where(owner == lax.broadcasted_iota(jnp.int32, s.shape, 0), s, NEG_BIG)
    s_new = lax.dot_general(q, kn_ref[0], (((1,), (1,)), ((), ())), preferred_element_type=F32)
    causal = lax.broadcasted_iota(jnp.int32, s_new.shape, 1) <= lax.broadcasted_iota(jnp.int32, s_new.shape, 0)
    s_new = jnp.where(causal, s_new, NEG_BIG)
    m = jnp.maximum(jnp.max(s, axis=-1, keepdims=True), jnp.max(s_new, axis=-1, keepdims=True))
    p = jnp.exp2(s - m)
    p_new = jnp.exp2(s_new - m)
    l = jnp.sum(p, axis=-1, keepdims=True) + jnp.sum(p_new, axis=-1, keepdims=True)
    acc = jnp.dot(p, vv, preferred_element_type=F32) + jnp.dot(p_new, vn_ref[0], preferred_element_type=F32)
    o_ref[0] = (acc / l) * _silu(ga_ref[0])


def _moba_sample(zs3, sel_flat, page_table, cache_k, cache_v, layer, n_heads, hd, q_col0, k_col0, v_col0, ga_col0):
    bsz, s_len, _ = zs3.shape
    page = cache_k.shape[2]
    assert MOBA_BLOCK % page == 0 and (page_table.shape[1] * page) % MOBA_BLOCK == 0 and s_len <= MOBA_BLOCK
    assert n_heads >= SAMPLE_SLOTS - 1
    rows = s_len * MOBA_TOP_K * MOBA_BLOCK
    qc, kc, vc, gc = (c // hd for c in (q_col0, k_col0, v_col0, ga_col0))
    tok = lambda c0: pl.BlockSpec((1, s_len, hd), lambda b, h, pt, sel: (b, 0, c0 + h))
    return pl.pallas_call(
        functools.partial(_moba_sample_kernel, layer=layer, scale=hd ** -0.5, n_heads=n_heads),
        grid_spec=pltpu.PrefetchScalarGridSpec(
            num_scalar_prefetch=2,
            grid=(bsz, n_heads),
            in_specs=[tok(qc), tok(kc), tok(vc), tok(gc),
                      pl.BlockSpec(memory_space=pl.ANY), pl.BlockSpec(memory_space=pl.ANY)],
            out_specs=pl.BlockSpec((1, s_len, hd), lambda b, h, pt, sel: (b, 0, h)),
            scratch_shapes=[
                pltpu.VMEM((SAMPLE_SLOTS, rows, hd), F32),
                pltpu.VMEM((SAMPLE_SLOTS, rows, hd), F32),
                pltpu.SemaphoreType.DMA((SAMPLE_SLOTS, 2)),
            ],
        ),
        out_shape=jax.ShapeDtypeStruct((bsz, s_len, n_heads * hd), F32),
        compiler_params=_params(("arbitrary", "arbitrary"), 2 * SAMPLE_SLOTS * rows * hd * 4 + 8 * rows * 8 * 4),
        name="moba_sample",
    )(page_table, sel_flat, zs3, zs3, zs3, zs3, cache_k, cache_v)


def _conv_sample_kernel(cv_ref, cg_ref, st_ref, w_ref, b_ref, lg_ref, lb_ref, u_ref, nst_ref, xp, ybuf):
    s_len = cv_ref.shape[1]
    n_ctx = st_ref.shape[1]
    cw = w_ref.shape[0]
    xp[pl.ds(0, n_ctx), :] = st_ref[0]
    xp[pl.ds(n_ctx, s_len), :] = cv_ref[0] * jax.nn.sigmoid(cg_ref[0])
    w = w_ref[...]
    for t in range(s_len):
        ybuf[pl.ds(t, 1), :] = jnp.sum(w * xp[pl.ds(t, cw), :], axis=0, keepdims=True)
    y = ybuf[pl.ds(0, s_len), :] + b_ref[...]
    mu = jnp.mean(y, axis=-1, keepdims=True)
    yc = y - mu
    var = jnp.mean(yc * yc, axis=-1, keepdims=True)
    yn = yc * lax.rsqrt(var + LN_EPS) * lg_ref[...] + lb_ref[...]
    u_ref[0] = _silu(yn)
    nst_ref[0] = xp[pl.ds(s_len, n_ctx), :]


def _conv_sample(zs3, cv_col0, cg_col0, state, w_dw, b_dw, ln_g, ln_b):
    bsz, s_len, _ = zs3.shape
    n_ctx, c = state.shape[1:]
    cw = w_dw.shape[0]
    assert n_ctx == cw - 1
    rows = -(-(n_ctx + s_len) // 8) * 8
    return pl.pallas_call(
        _conv_sample_kernel,
        grid=(bsz,),
        in_specs=[
            pl.BlockSpec((1, s_len, c), lambda b: (b, 0, cv_col0 // c)),
            pl.BlockSpec((1, s_len, c), lambda b: (b, 0, cg_col0 // c)),
            pl.BlockSpec((1, n_ctx, c), lambda b: (b, 0, 0)),
            pl.BlockSpec((cw, c), lambda b: (0, 0)),
            pl.BlockSpec((1, c), lambda b: (0, 0)),
            pl.BlockSpec((1, c), lambda b: (0, 0)),
            pl.BlockSpec((1, c), lambda b: (0, 0)),
        ],
        out_specs=[pl.BlockSpec((1, s_len, c), lambda b: (b, 0, 0)), pl.BlockSpec((1, n_ctx, c), lambda b: (b, 0, 0))],
        out_shape=[jax.ShapeDtypeStruct((bsz, s_len, c), F32), jax.ShapeDtypeStruct((bsz, n_ctx, c), F32)],
        scratch_shapes=[pltpu.VMEM((rows, c), F32), pltpu.VMEM((8, c), F32)],
        compiler_params=_params(("parallel",), 16 * rows * c * 4),
        name="conv_sample",
    )(zs3, zs3, state, w_dw, b_dw.reshape(1, c), ln_g.reshape(1, c), ln_b.reshape(1, c))


def kernel(x_prompt, x_sample, cache_k, cache_v, state_conv, page_table, norm_in_g, w_in, w_dw, b_dw, ln_g, ln_b,
           w_pw2, w_out, norm_f_g):
    bp, t, d = x_prompt.shape
    bs, s_len, _ = x_sample.shape
    depth = w_in.shape[0]
    n_heads, hd = cache_k.shape[3:]
    att_w = n_heads * hd
    c = state_conv.shape[-1]
    n_ctx = state_conv.shape[2]
    assert att_w == c, "column-block addressing of the combined projection assumes equal group widths"
    q0, k0, v0, ga0 = 0, att_w, 2 * att_w, 3 * att_w
    cv0, cg0, gc0 = 4 * att_w, 4 * att_w + c, 4 * att_w + 2 * c

    xp = x_prompt.reshape(bp * t, d)
    xs = x_sample.reshape(bs * s_len, d)
    outs = {name: [] for name in ("kp", "vp", "cp", "ks", "vs", "cs")}
    for l in range(depth):
        w_in_b = w_in[l].astype(BF16)

        hp = _rms_norm(xp, norm_in_g[l], BF16)
        zp, kmean_s = _matmul_kmean(hp, w_in_b, cache_k, l, page_table)
        z_att, z_conv, kp, vp, cp = [], [], [], [], []
        for bi in range(bp):
            zb = zp[bi * t:(bi + 1) * t] if bp > 1 else zp
            kmean = _kmean_prompt(zb, k0, att_w)
            sel_p = _gate_prompt(zb, kmean, n_heads, hd, q0)
            z_att.append(_moba_prompt(zb, sel_p, n_heads, hd, q0, k0, v0, ga0))
            u, tail = _conv_prompt(zb, cv0, cg0, c, jnp.zeros((n_ctx, c), F32), w_dw[l], b_dw[l], ln_g[l], ln_b[l])
            z_conv.append(_matmul(u, w_pw2[l], 0, c, BF16, gate=zb, gate_col0=gc0))
            kp.append(zb[:, k0:k0 + att_w].reshape(t, n_heads, hd))
            vp.append(zb[:, v0:v0 + att_w].reshape(t, n_heads, hd))
            cp.append(tail)
        z_att = z_att[0] if bp == 1 else jnp.concatenate(z_att, axis=0)
        z_conv = z_conv[0] if bp == 1 else jnp.concatenate(z_conv, axis=0)
        xp = _outproj(z_att, z_conv, w_out[l], xp)
        outs["kp"].append(jnp.stack(kp)); outs["vp"].append(jnp.stack(vp)); outs["cp"].append(jnp.stack(cp))

        hs = _rms_norm(xs, norm_in_g[l], BF16)
        zs = _matmul(hs, w_in_b, 0, w_in_b.shape[1], F32)
        zs3 = zs.reshape(bs, s_len, zs.shape[1])
        sel = _gate_sample(zs3, kmean_s, n_heads, hd, q0)[:, :, :MOBA_TOP_K].reshape(-1)
        z_att_s = _moba_sample(zs3, sel, page_table, cache_k, cache_v, l, n_heads, hd, q0, k0, v0, ga0)
        u_s, new_state = _conv_sample(zs3, cv0, cg0, state_conv[l], w_dw[l], b_dw[l], ln_g[l], ln_b[l])
        z_conv_s = _matmul(u_s.reshape(bs * s_len, c), w_pw2[l], 0, c, BF16, gate=zs, gate_col0=gc0)
        xs = _outproj(z_att_s.reshape(bs * s_len, att_w), z_conv_s, w_out[l], xs)
        outs["ks"].append(zs[:, k0:k0 + att_w].reshape(bs, s_len, n_heads, hd))
        outs["vs"].append(zs[:, v0:v0 + att_w].reshape(bs, s_len, n_heads, hd))
        outs["cs"].append(new_state)

    y_prompt = _rms_norm(xp, norm_f_g, F32).reshape(bp, t, d)
    y_sample = _rms_norm(xs, norm_f_g, F32).reshape(bs, s_len, d)
    return (y_prompt, y_sample, jnp.stack(outs["kp"]), jnp.stack(outs["vp"]), jnp.stack(outs["cp"]),
            jnp.stack(outs["ks"]), jnp.stack(outs["vs"]), jnp.stack(outs["cs"]))
```

```python
import functools
import math

import jax
import jax.numpy as jnp
from jax import lax
from jax.experimental import pallas as pl
from jax.experimental.pallas import tpu as pltpu

MOBA_BLOCK = 256
MOBA_TOP_K = 3
RMS_EPS = 1e-6
LN_EPS = 1e-5
LOG2E = math.log2(math.e)
NEG_BIG = -1e30
POS_BIG = 1e30
CONV_HALO = 32
SUBLANES = 8
CONV_ROWS = 32
CONV_LANES = 256
ONES_ROWS = 16
SAMPLE_SLOTS = 3
PROMPT_UNROLL = 8
PROMPT_QBLOCKS = 2
V7X_SCOPED_VMEM_BYTES = 60000 * 1024
F32 = jnp.float32
BF16 = jnp.bfloat16


def _params(semantics, vmem_bytes):
    limit = int(min(V7X_SCOPED_VMEM_BYTES, max(vmem_bytes, 16 * 1024 * 1024)))
    return pltpu.CompilerParams(dimension_semantics=semantics, vmem_limit_bytes=limit)


def _tile(n, pref, unit):
    if n <= pref:
        return n
    t = (pref // unit) * unit
    while t >= unit:
        if n % t == 0:
            return t
        t -= unit
    raise ValueError(f"no tile for {n} (pref {pref}, unit {unit})")


def _silu(x):
    return x * jax.nn.sigmoid(x)


def _rms_kernel(x_ref, g_ref, o_ref):
    x = x_ref[...]
    ms = jnp.mean(x * x, axis=-1, keepdims=True)
    y = x * lax.rsqrt(ms + RMS_EPS)
    o_ref[...] = (y * g_ref[...]).astype(o_ref.dtype)


def _rms_norm(x, g, out_dtype):
    m, d = x.shape
    tm = _tile(m, 256, 8)
    blk = tm * d * 4
    return pl.pallas_call(
        _rms_kernel,
        grid=(m // tm,),
        in_specs=[pl.BlockSpec((tm, d), lambda i: (i, 0)), pl.BlockSpec((1, d), lambda i: (0, 0))],
        out_specs=pl.BlockSpec((tm, d), lambda i: (i, 0)),
        out_shape=jax.ShapeDtypeStruct((m, d), out_dtype),
        compiler_params=_params(("parallel",), 6 * blk),
        name="rms_norm",
    )(x, g.reshape(1, d))


def _mm_kernel(*refs, gated):
    if gated:
        a_ref, b_ref, g_ref, o_ref = refs
    else:
        a_ref, b_ref, o_ref = refs
    acc = jnp.dot(a_ref[...].astype(BF16), b_ref[...].astype(BF16), preferred_element_type=F32)
    if gated:
        acc = acc * _silu(g_ref[...])
    o_ref[...] = acc.astype(o_ref.dtype)


def _matmul(a, b, col0, n, out_dtype, gate=None, gate_col0=0):
    m, k = a.shape
    tm = _tile(m, 1024, 8)
    tn = _tile(n, 512, 128)
    assert col0 % tn == 0 and gate_col0 % tn == 0
    cb, gb = col0 // tn, gate_col0 // tn
    in_specs = [pl.BlockSpec((tm, k), lambda i, j: (i, 0)), pl.BlockSpec((k, tn), lambda i, j: (0, j + cb))]
    args = [a, b]
    if gate is not None:
        in_specs.append(pl.BlockSpec((tm, tn), lambda i, j: (i, j + gb)))
        args.append(gate)
    vmem = 2 * (tm * k * a.dtype.itemsize + k * tn * b.dtype.itemsize + 3 * tm * tn * 4) + (tm + tn) * k * 2
    return pl.pallas_call(
        functools.partial(_mm_kernel, gated=gate is not None),
        grid=(m // tm, n // tn),
        in_specs=in_specs,
        out_specs=pl.BlockSpec((tm, tn), lambda i, j: (i, j)),
        out_shape=jax.ShapeDtypeStruct((m, n), out_dtype),
        compiler_params=_params(("parallel", "parallel"), vmem),
        name="matmul",
    )(*args)


def _mm_kmean_kernel(pt_ref, a_ref, b_ref, ck_ref, o_ref, km_ref, pbuf, sem, *,
                     layer, n_blocks, seq_blocks, blocks_per_step, n_j):
    step = pl.program_id(0) * n_j + pl.program_id(1)
    ppb = pbuf.shape[1] // blocks_per_step
    n_heads, hd = pbuf.shape[-2:]
    n_active = -(-n_blocks // blocks_per_step)

    def copies(st, slot):
        out = []
        for r in range(blocks_per_step):
            g = jnp.minimum(st * blocks_per_step + r, n_blocks - 1)
            for p in range(ppb):
                pg = pt_ref[g // seq_blocks, (g % seq_blocks) * ppb + p]
                out.append(pltpu.make_async_copy(ck_ref.at[layer, pg], pbuf.at[slot, r * ppb + p], sem.at[slot]))
        return out

    @pl.when(step == 0)
    def _():
        for cp in copies(0, 0):
            cp.start()

    @pl.when(step + 1 < n_active)
    def _():
        for cp in copies(step + 1, (step + 1) % 2):
            cp.start()

    slot = step % 2

    @pl.when(step < n_active)
    def _():
        for cp in copies(step, slot):
            cp.wait()

    means = []
    for r in range(blocks_per_step):
        tot = jnp.sum(pbuf[slot, r * ppb], axis=0)
        for p in range(1, ppb):
            tot = tot + jnp.sum(pbuf[slot, r * ppb + p], axis=0)
        means.append(tot * (1.0 / MOBA_BLOCK))
    o_ref[...] = jnp.dot(a_ref[...], b_ref[...], preferred_element_type=F32)

    @pl.when(step < n_active)
    def _():
        for r in range(blocks_per_step):
            for h in range(n_heads):
                km_ref[r, :, pl.ds(h * hd, hd)] = means[r][h:h + 1, :]


def _matmul_kmean(a, b, cache_k, layer, page_table):
    m, k = a.shape
    n = b.shape[1]
    _, _, page, n_heads, hd = cache_k.shape
    bsz, n_pages = page_table.shape
    ppb = MOBA_BLOCK // page
    seq_blocks = n_pages // ppb
    n_blocks = bsz * seq_blocks
    tm = _tile(m, 1024, 8)
    tn = _tile(n, 512, 128)
    n_i, n_j = m // tm, n // tn
    blocks_per_step = -(-n_blocks // (n_i * n_j))
    n_active = -(-n_blocks // blocks_per_step)
    pages = blocks_per_step * ppb
    vmem = 2 * (tm * k * 2 + k * tn * 2 + 2 * tm * tn * 4) + 2 * pages * page * n_heads * hd * 4
    z, km = pl.pallas_call(
        functools.partial(_mm_kmean_kernel, layer=layer, n_blocks=n_blocks, seq_blocks=seq_blocks,
                          blocks_per_step=blocks_per_step, n_j=n_j),
        grid_spec=pltpu.PrefetchScalarGridSpec(
            num_scalar_prefetch=1,
            grid=(n_i, n_j),
            in_specs=[
                pl.BlockSpec((tm, k), lambda i, j, pt: (i, 0)),
                pl.BlockSpec((k, tn), lambda i, j, pt: (0, j)),
                pl.BlockSpec(memory_space=pl.ANY),
            ],
            out_specs=[
                pl.BlockSpec((tm, tn), lambda i, j, pt: (i, j)),
                pl.BlockSpec((blocks_per_step, 1, n_heads * hd),
                             lambda i, j, pt: (jnp.minimum(i * n_j + j, n_active - 1), 0, 0)),
            ],
            scratch_shapes=[
                pltpu.VMEM((2, pages, page, n_heads, hd), F32),
                pltpu.SemaphoreType.DMA((2,)),
            ],
        ),
        out_shape=[
            jax.ShapeDtypeStruct((m, n), F32),
            jax.ShapeDtypeStruct((n_active * blocks_per_step, 1, n_heads * hd), F32),
        ],
        compiler_params=_params(("arbitrary", "arbitrary"), vmem),
        name="matmul_kmean",
    )(page_table, a, b, cache_k)
    return z, km[:n_blocks].reshape(bsz, seq_blocks, 1, n_heads * hd)


def _outproj_kernel(a1_ref, a2_ref, w1_ref, w2_ref, x_ref, o_ref):
    acc = jnp.dot(a1_ref[...].astype(BF16), w1_ref[...].astype(BF16), preferred_element_type=F32)
    acc += jnp.dot(a2_ref[...].astype(BF16), w2_ref[...].astype(BF16), preferred_element_type=F32)
    o_ref[...] = x_ref[...] + acc


def _outproj(a1, a2, w, x):
    m, k = a1.shape
    n = x.shape[1]
    assert a2.shape[1] == k and w.shape[0] == 2 * k
    tm = _tile(m, 1024, 8)
    tn = _tile(n, 512, 128)
    vmem = (2 * (tm * k * (a1.dtype.itemsize + a2.dtype.itemsize) + 2 * k * tn * w.dtype.itemsize + 3 * tm * tn * 4)
            + 2 * (tm + tn) * k * 2)
    return pl.pallas_call(
        _outproj_kernel,
        grid=(m // tm, n // tn),
        in_specs=[
            pl.BlockSpec((tm, k), lambda i, j: (i, 0)),
            pl.BlockSpec((tm, k), lambda i, j: (i, 0)),
            pl.BlockSpec((k, tn), lambda i, j: (0, j)),
            pl.BlockSpec((k, tn), lambda i, j: (1, j)),
            pl.BlockSpec((tm, tn), lambda i, j: (i, j)),
        ],
        out_specs=pl.BlockSpec((tm, tn), lambda i, j: (i, j)),
        out_shape=jax.ShapeDtypeStruct((m, n), F32),
        compiler_params=_params(("parallel", "parallel"), vmem),
        name="outproj",
    )(a1, a2, w, w, x)


def _topk_first_indices(gate, blk, axis=-1):
    nb = gate.shape[axis]
    firsts = []
    for _ in range(MOBA_TOP_K):
        mx = jnp.max(gate, axis=axis, keepdims=True)
        first = jnp.min(jnp.where(gate == mx, blk, float(nb)), axis=axis, keepdims=True)
        firsts.append(first)
        gate = jnp.where(blk == first, -jnp.inf, gate)
    return firsts


def _gate_prompt_kernel(q_ref, k_ref, o_ref, km_sc):
    j = pl.program_id(1)
    tq = q_ref.shape[0]
    nb = km_sc.shape[0]

    @pl.when(j == 0)
    def _():
        for n in range(nb):
            km_sc[pl.ds(n, 1), :] = (jnp.sum(k_ref[pl.ds(n * MOBA_BLOCK, MOBA_BLOCK), :], axis=0, keepdims=True)
                                     * (1.0 / MOBA_BLOCK))

    gate = lax.dot_general(km_sc[...], q_ref[...], (((1,), (1,)), ((), ())),
                           precision=lax.Precision.HIGHEST, preferred_element_type=F32)
    blk = lax.broadcasted_iota(jnp.int32, (nb, tq), 0)
    own = (lax.broadcasted_iota(jnp.int32, (nb, tq), 1) + j * tq) // MOBA_BLOCK
    past = blk < own
    blk_f = blk.astype(F32)
    firsts = _topk_first_indices(jnp.where(past, gate, -jnp.inf), blk_f, axis=0)
    sel = jnp.zeros((nb, tq), F32)
    for first in firsts:
        sel = jnp.maximum(sel, jnp.where(blk_f == first, 1.0, 0.0))
    sel = jnp.where(past, sel, 0.0)
    o_ref[0] = jnp.where(blk == own, 1.0, sel)


def _gate_prompt(z, n_heads, hd, q_col0, k_col0):
    t = z.shape[0]
    assert t % MOBA_BLOCK == 0
    nb = t // MOBA_BLOCK
    tq = _tile(t, 2048, MOBA_BLOCK)
    return pl.pallas_call(
        _gate_prompt_kernel,
        grid=(n_heads, t // tq),
        in_specs=[
            pl.BlockSpec((tq, hd), lambda h, j: (j, q_col0 // hd + h)),
            pl.BlockSpec((t, hd), lambda h, j: (0, k_col0 // hd + h)),
        ],
        out_specs=pl.BlockSpec((1, nb, tq), lambda h, j: (h, 0, j)),
        out_shape=jax.ShapeDtypeStruct((n_heads, nb, t), F32),
        scratch_shapes=[pltpu.VMEM((nb, hd), F32)],
        compiler_params=_params(("parallel", "arbitrary"), 4 * (tq + t) * hd * 4 + 24 * nb * tq * 4),
        name="gate_prompt",
    )(z, z)


def _moba_prompt_kernel(q_ref, qn_ref, k_ref, v_ref, sel_ref, ga_ref, o_ref, ko_ref, vo_ref, kb_sc, vt_sc, bias_sc,
                        s_sc, bm_sc, acc_sc, m_sc, slot_sc, kv_sem, *, scale):
    i = pl.program_id(1)
    bq, hd = q_ref.shape
    qblocks = bq // MOBA_BLOCK
    n_chunk_total = vt_sc.shape[0]
    nt = (((1,), (1,)), ((), ()))
    chunk = PROMPT_UNROLL * MOBA_BLOCK

    head = pl.program_id(0)
    kv_out = (pltpu.make_async_copy(k_ref, ko_ref.at[:, head, :], kv_sem.at[0]),
              pltpu.make_async_copy(v_ref, vo_ref.at[:, head, :], kv_sem.at[1]))

    @pl.when(i == 0)
    def _():
        for cp in kv_out:
            cp.start()

    @pl.when(i == pl.num_programs(1) - 1)
    def _():
        for cp in kv_out:
            cp.wait()

    @pl.when(i == 0)
    def _():
        kb_sc[...] = k_ref[...].astype(BF16)
        key = lax.broadcasted_iota(jnp.int32, (MOBA_BLOCK, bq), 0)
        qry = lax.broadcasted_iota(jnp.int32, (MOBA_BLOCK, bq), 1)
        bias_sc[0] = jnp.zeros((MOBA_BLOCK, bq), F32)
        for r in range(qblocks):
            bias_sc[1 + r] = jnp.where(qry - r * MOBA_BLOCK >= key, 0.0, NEG_BIG)

        def stage(c, carry):
            for u in range(PROMPT_UNROLL):
                off = pl.multiple_of((c * PROMPT_UNROLL + u) * MOBA_BLOCK, MOBA_BLOCK)
                vt_sc[c, pl.ds(0, hd), pl.ds(u * MOBA_BLOCK, MOBA_BLOCK)] = (
                    v_ref[pl.ds(off, MOBA_BLOCK), :].T.astype(BF16))
            vt_sc[c, pl.ds(hd, ONES_ROWS), :] = jnp.ones((ONES_ROWS, chunk), BF16)
            return carry

        lax.fori_loop(0, n_chunk_total, stage, 0)

    qb = (q_ref[...] * (scale * LOG2E)).astype(BF16)
    qb_next = (qn_ref[...] * (scale * LOG2E)).astype(BF16)
    first_own = i * qblocks
    first_own_next = jnp.minimum(i + 1, pl.num_programs(1) - 1) * qblocks
    n_chunks = (first_own + qblocks - 1) // PROMPT_UNROLL + 1

    def scores_into(slot, q_bf16, tile_first_own, c):
        off = pl.multiple_of(c * chunk, chunk)
        s = lax.dot_general(kb_sc[pl.ds(off, chunk), :], q_bf16, nt, preferred_element_type=F32)
        for u in range(PROMPT_UNROLL):
            r = c * PROMPT_UNROLL + u - tile_first_own
            which = jnp.where((r >= 0) & (r < qblocks), r + 1, 0)
            su = s[u * MOBA_BLOCK:(u + 1) * MOBA_BLOCK] + bias_sc[which]
            s_sc[slot, pl.ds(u * MOBA_BLOCK, MOBA_BLOCK), :] = su
            bm_sc[slot, pl.ds(u, 1), :] = jnp.max(su, axis=0, keepdims=True)

    @pl.when(i == 0)
    def _():
        slot_sc[0] = 0
        scores_into(0, qb, first_own, 0)

    slot0 = slot_sc[0]
    acc_sc[...] = jnp.zeros(acc_sc.shape, F32)
    m_sc[...] = jnp.full(m_sc.shape, NEG_BIG, F32)

    def trip(c, carry):
        cur = (slot0 + c) % 2
        m_old = m_sc[...]
        picked = sel_ref[0, pl.ds(pl.multiple_of(c * PROMPT_UNROLL, PROMPT_UNROLL), PROMPT_UNROLL), :] > 0.5
        m_new = jnp.maximum(m_old, jnp.max(jnp.where(picked, bm_sc[cur], NEG_BIG), axis=0, keepdims=True))
        ps = []
        for u in range(PROMPT_UNROLL):
            m_eff = jnp.where(picked[u:u + 1], m_new, POS_BIG)
            ps.append(jnp.exp2(s_sc[cur, pl.ds(u * MOBA_BLOCK, MOBA_BLOCK), :] - m_eff).astype(BF16))
        pv = jnp.dot(vt_sc[c], jnp.concatenate(ps, axis=0), preferred_element_type=F32)
        acc_sc[...] = acc_sc[...] * jnp.exp2(m_old - m_new) + pv
        m_sc[...] = m_new
        last = c + 1 == n_chunks
        scores_into(1 - cur, jnp.where(last, qb_next, qb), jnp.where(last, first_own_next, first_own),
                    jnp.where(last, 0, c + 1))
        return carry

    lax.fori_loop(0, n_chunks, trip, 0)
    slot_sc[0] = (slot0 + n_chunks) % 2

    acc = acc_sc[...]
    out = (acc[:hd] / acc[hd:hd + 1]).T
    o_ref[...] = (out * _silu(ga_ref[...])).astype(o_ref.dtype)


def _moba_prompt(z, sel, n_heads, hd, q_col0, k_col0, v_col0, ga_col0):
    t = z.shape[0]
    chunk = PROMPT_UNROLL * MOBA_BLOCK
    bq = PROMPT_QBLOCKS * MOBA_BLOCK
    assert t % chunk == 0 and t % bq == 0
    nb = t // MOBA_BLOCK
    qc, kc, vc, gc = (c // hd for c in (q_col0, k_col0, v_col0, ga_col0))
    scratch = [
        pltpu.VMEM((t, hd), BF16),
        pltpu.VMEM((t // chunk, hd + ONES_ROWS, chunk), BF16),
        pltpu.VMEM((1 + PROMPT_QBLOCKS, MOBA_BLOCK, bq), F32),
        pltpu.VMEM((2, chunk, bq), F32),
        pltpu.VMEM((2, PROMPT_UNROLL, bq), F32),
        pltpu.VMEM((hd + ONES_ROWS, bq), F32),
        pltpu.VMEM((1, bq), F32),
        pltpu.SMEM((1,), jnp.int32),
        pltpu.SemaphoreType.DMA((2,)),
    ]
    n_tiles = t // bq
    vmem = (2 * (2 * t * hd * 4 + 4 * bq * hd * 4 + nb * bq * 4)
            + t * hd * 2 + (t // chunk) * (hd + ONES_ROWS) * chunk * 2 + (1 + PROMPT_QBLOCKS) * MOBA_BLOCK * bq * 4
            + 5 * chunk * bq * 4)
    return pl.pallas_call(
        functools.partial(_moba_prompt_kernel, scale=hd ** -0.5),
        grid=(n_heads, n_tiles),
        in_specs=[
            pl.BlockSpec((bq, hd), lambda h, i: (i, qc + h)),
            pl.BlockSpec((bq, hd), lambda h, i: (jnp.minimum(i + 1, n_tiles - 1), qc + h)),
            pl.BlockSpec((t, hd), lambda h, i: (0, kc + h)),
            pl.BlockSpec((t, hd), lambda h, i: (0, vc + h)),
            pl.BlockSpec((1, nb, bq), lambda h, i: (h, 0, i)),
            pl.BlockSpec((bq, hd), lambda h, i: (i, gc + h)),
        ],
        out_specs=[
            pl.BlockSpec((bq, hd), lambda h, i: (i, h)),
            pl.BlockSpec(memory_space=pl.ANY),
            pl.BlockSpec(memory_space=pl.ANY),
        ],
        out_shape=[
            jax.ShapeDtypeStruct((t, n_heads * hd), BF16),
            jax.ShapeDtypeStruct((t, n_heads, hd), F32),
            jax.ShapeDtypeStruct((t, n_heads, hd), F32),
        ],
        scratch_shapes=scratch,
        compiler_params=_params(("arbitrary", "arbitrary"), vmem),
        name="moba_prompt",
    )(z, z, z, z, sel, z)


def _conv_prompt_kernel(cv_ref, cvp_ref, cg_ref, cgp_ref, ctx_ref, w_ref, b_ref, lg_ref, lb_ref,
                        u_ref, tail_ref, gbuf, shifted, ybuf, wb):
    i = pl.program_id(0)
    tt = cv_ref.shape[0]
    cw = w_ref.shape[0]
    glu = cv_ref[...] * jax.nn.sigmoid(cg_ref[...])
    gbuf[pl.ds(CONV_HALO, tt), :] = glu

    @pl.when(i == 0)
    def _():
        gbuf[pl.ds(0, CONV_HALO), :] = ctx_ref[...]

    @pl.when(i > 0)
    def _():
        gbuf[pl.ds(0, CONV_HALO), :] = cvp_ref[...] * jax.nn.sigmoid(cgp_ref[...])

    span = shifted.shape[1]
    for s in range(1, SUBLANES):
        shifted[s - 1] = gbuf[pl.ds(s, span), :]
    base = CONV_HALO - (cw - 1)

    @pl.when(i == 0)
    def _():
        for k in range(cw):
            wb[k] = jnp.broadcast_to(w_ref[k:k + 1, :], wb.shape[1:])

    def rows_chunk(j, carry):
        r0 = pl.multiple_of(j * CONV_ROWS, CONV_ROWS)
        for c0 in range(0, ybuf.shape[1], CONV_LANES):
            lanes = pl.ds(c0, CONV_LANES)
            acc = None
            for s in range(SUBLANES):
                wholes = [w for w in range((base + cw - 1) // SUBLANES + 1) if 0 <= w * SUBLANES + s - base < cw]
                rows = pl.ds(r0 + wholes[0] * SUBLANES, CONV_ROWS + (wholes[-1] - wholes[0]) * SUBLANES)
                xs = gbuf[rows, lanes] if s == 0 else shifted[s - 1, rows, lanes]
                for w in wholes:
                    k = w * SUBLANES + s - base
                    x = xs[(w - wholes[0]) * SUBLANES:(w - wholes[0]) * SUBLANES + CONV_ROWS]
                    wk = jnp.concatenate([wb[k, :, lanes]] * (CONV_ROWS // SUBLANES), axis=0)
                    acc = wk * x if acc is None else acc + wk * x
            ybuf[pl.ds(r0, CONV_ROWS), lanes] = acc
        return carry

    lax.fori_loop(0, tt // CONV_ROWS, rows_chunk, 0)
    y = ybuf[...] + b_ref[...]
    mu = jnp.mean(y, axis=-1, keepdims=True)
    yc = y - mu
    var = jnp.mean(yc * yc, axis=-1, keepdims=True)
    yn = yc * lax.rsqrt(var + LN_EPS) * lg_ref[...] + lb_ref[...]
    u_ref[...] = _silu(yn).astype(u_ref.dtype)
    tail_ref[...] = gbuf[pl.ds(tt, CONV_HALO), :]


def _conv_prompt(z, cv_col0, cg_col0, c, ctx, w_dw, b_dw, ln_g, ln_b):
    t = z.shape[0]
    cw = w_dw.shape[0]
    n_ctx = cw - 1
    assert n_ctx <= CONV_HALO
    tt = _tile(t, 256, CONV_HALO)
    assert t % tt == 0 and tt % CONV_HALO == 0
    r = tt // CONV_HALO
    cvb, cgb = cv_col0 // c, cg_col0 // c
    ctx_pad = jnp.concatenate([jnp.zeros((CONV_HALO - n_ctx, c), F32), ctx], axis=0)
    prev = lambda i: jnp.maximum(i * r - 1, 0)
    assert tt % CONV_ROWS == 0 and CONV_ROWS % SUBLANES == 0 and CONV_HALO % SUBLANES == 0
    span = tt + CONV_HALO - SUBLANES
    vmem = (2 * (2 * tt * c * 4 + 3 * CONV_HALO * c * 4 + tt * c * 2 + (cw + 3) * c * 4)
            + ((tt + CONV_HALO) + (SUBLANES - 1) * span + 5 * tt) * c * 4)
    u, tail = pl.pallas_call(
        _conv_prompt_kernel,
        grid=(t // tt,),
        in_specs=[
            pl.BlockSpec((tt, c), lambda i: (i, cvb)),
            pl.BlockSpec((CONV_HALO, c), lambda i: (prev(i), cvb)),
            pl.BlockSpec((tt, c), lambda i: (i, cgb)),
            pl.BlockSpec((CONV_HALO, c), lambda i: (prev(i), cgb)),
            pl.BlockSpec((CONV_HALO, c), lambda i: (0, 0)),
            pl.BlockSpec((cw, c), lambda i: (0, 0)),
            pl.BlockSpec((1, c), lambda i: (0, 0)),
            pl.BlockSpec((1, c), lambda i: (0, 0)),
            pl.BlockSpec((1, c), lambda i: (0, 0)),
        ],
        out_specs=[pl.BlockSpec((tt, c), lambda i: (i, 0)), pl.BlockSpec((CONV_HALO, c), lambda i: (0, 0))],
        out_shape=[jax.ShapeDtypeStruct((t, c), BF16), jax.ShapeDtypeStruct((CONV_HALO, c), F32)],
        scratch_shapes=[
            pltpu.VMEM((CONV_HALO + tt, c), F32),
            pltpu.VMEM((SUBLANES - 1, span, c), F32),
            pltpu.VMEM((tt, c), F32),
            pltpu.VMEM((cw, SUBLANES, c), F32),
        ],
        compiler_params=_params(("arbitrary",), vmem),
        name="conv_prompt",
    )(z, z, z, z, ctx_pad, w_dw, b_dw.reshape(1, c), ln_g.reshape(1, c), ln_b.reshape(1, c))
    return u, tail[CONV_HALO - n_ctx:]


def _gate_sample_kernel(q_ref, km_ref, o_ref, *, n_heads, hd):
    s = q_ref.shape[1]
    km = km_ref[0, :, 0, :]
    nbp, w = km.shape
    rows = [jnp.broadcast_to(q_ref[0, t:t + 1, :], (n_heads, w)) for t in range(s)]
    qbd = jnp.concatenate(rows, axis=0)
    r = lax.broadcasted_iota(jnp.int32, qbd.shape, 0) % n_heads
    c = lax.broadcasted_iota(jnp.int32, qbd.shape, 1) // hd
    qbd = jnp.where(r == c, qbd, 0.0)
    gate = lax.dot_general(qbd, km, (((1,), (1,)), ((), ())),
                           precision=lax.Precision.HIGHEST, preferred_element_type=F32)
    blk = lax.broadcasted_iota(jnp.int32, gate.shape, 1).astype(F32)
    firsts = _topk_first_indices(gate, blk)
    lane = lax.broadcasted_iota(jnp.int32, o_ref.shape[1:], 1)
    out = jnp.zeros(o_ref.shape[1:], F32)
    for j, first in enumerate(firsts):
        out = jnp.where(lane == j, first, out)
    o_ref[0] = out.astype(jnp.int32)


def _gate_sample(zs3, kmean, n_heads, hd, q_col0):
    bsz, s, _ = zs3.shape
    nbp = kmean.shape[1]
    w = n_heads * hd
    assert nbp >= MOBA_TOP_K and n_heads % 8 == 0 and q_col0 % w == 0
    return pl.pallas_call(
        functools.partial(_gate_sample_kernel, n_heads=n_heads, hd=hd),
        grid=(bsz,),
        in_specs=[
            pl.BlockSpec((1, s, w), lambda b: (b, 0, q_col0 // w)),
            pl.BlockSpec((1, nbp, 1, w), lambda b: (b, 0, 0, 0)),
        ],
        out_specs=pl.BlockSpec((1, s * n_heads, 128), lambda b: (b, 0, 0)),
        out_shape=jax.ShapeDtypeStruct((bsz, s * n_heads, 128), jnp.int32),
        compiler_params=_params(("parallel",), 8 * (s * n_heads + nbp * 8) * w * 4),
        name="gate_sample",
    )(zs3, kmean)


def _moba_sample_kernel(pt_ref, sel_ref, q_ref, kn_ref, vn_ref, ga_ref, ck_ref, cv_ref, o_ref,
                        kbuf, vbuf, sem, *, layer, scale, n_heads):
    b = pl.program_id(0)
    h = pl.program_id(1)
    n_b = pl.num_programs(0)
    s_len, hd = q_ref.shape[1:]
    page = ck_ref.shape[2]
    ppb = MOBA_BLOCK // page
    step = b * n_heads + h
    n_steps = n_b * n_heads

    def copies(bb, hh, slot):
        out = []
        for t in range(s_len):
            for j in range(MOBA_TOP_K):
                blk = sel_ref[((bb * s_len + t) * n_heads + hh) * MOBA_TOP_K + j]
                for r in range(ppb):
                    pg = pt_ref[bb, blk * ppb + r]
                    dst = pl.ds(((t * MOBA_TOP_K + j) * ppb + r) * page, page)
                    out.append(pltpu.make_async_copy(ck_ref.at[layer, pg, :, hh, :], kbuf.at[slot, dst, :],
                                                     sem.at[slot, 0]))
                    out.append(pltpu.make_async_copy(cv_ref.at[layer, pg, :, hh, :], vbuf.at[slot, dst, :],
                                                     sem.at[slot, 1]))
        return out

    def start_all(cps):
        for n, cp in enumerate(cps):
            cp.start(priority=n % 2)

    n_slots = kbuf.shape[0]
    ahead = n_slots - 1

    @pl.when(step == 0)
    def _():
        for st in range(ahead):
            if st < n_heads:
                start_all(copies(0, st, st))

    @pl.when(step + ahead < n_steps)
    def _():
        nxt = step + ahead
        start_all(copies(nxt // n_heads, nxt % n_heads, nxt % n_slots))

    slot = step % n_slots
    for cp in copies(b, h, slot):
        cp.wait()

    q = q_ref[0] * (scale * LOG2E)
    kk = kbuf[slot]
    vv = vbuf[slot]
    s = lax.dot_general(q, kk, (((1,), (1,)), ((), ())), preferred_element_type=F32)
    owner = lax.broadcasted_iota(jnp.int32, s.shape, 1) // (MOBA_TOP_K * MOBA_BLOCK)
    s = jnp.where(owner == lax.broadcasted_iota(jnp.int32, s.shape, 0), s, NEG_BIG)
    s_new = lax.dot_general(q, kn_ref[0], (((1,), (1,)), ((), ())), preferred_element_type=F32)
    causal = lax.broadcasted_iota(jnp.int32, s_new.shape, 1) <= lax.broadcasted_iota(jnp.int32, s_new.shape, 0)
    s_new = jnp.where(causal, s_new, NEG_BIG)
    m = jnp.maximum(jnp.max(s, axis=-1, keepdims=True), jnp.max(s_new, axis=-1, keepdims=True))
    p = jnp.exp2(s - m)
    p_new = jnp.exp2(s_new - m)
    l = jnp.sum(p, axis=-1, keepdims=True) + jnp.sum(p_new, axis=-1, keepdims=True)
    acc = jnp.dot(p, vv, preferred_element_type=F32) + jnp.dot(p_new, vn_ref[0], preferred_element_type=F32)
    o_ref[0] = (acc / l) * _silu(ga_ref[0])


def _moba_sample(zs3, sel_flat, page_table, cache_k, cache_v, layer, n_heads, hd, q_col0, k_col0, v_col0, ga_col0):
    bsz, s_len, _ = zs3.shape
    page = cache_k.shape[2]
    assert MOBA_BLOCK % page == 0 and (page_table.shape[1] * page) % MOBA_BLOCK == 0 and s_len <= MOBA_BLOCK
    assert n_heads >= SAMPLE_SLOTS - 1
    rows = s_len * MOBA_TOP_K * MOBA_BLOCK
    qc, kc, vc, gc = (c // hd for c in (q_col0, k_col0, v_col0, ga_col0))
    tok = lambda c0: pl.BlockSpec((1, s_len, hd), lambda b, h, pt, sel: (b, 0, c0 + h))
    return pl.pallas_call(
        functools.partial(_moba_sample_kernel, layer=layer, scale=hd ** -0.5, n_heads=n_heads),
        grid_spec=pltpu.PrefetchScalarGridSpec(
            num_scalar_prefetch=2,
            grid=(bsz, n_heads),
            in_specs=[tok(qc), tok(kc), tok(vc), tok(gc),
                      pl.BlockSpec(memory_space=pl.ANY), pl.BlockSpec(memory_space=pl.ANY)],
            out_specs=pl.BlockSpec((1, s_len, hd), lambda b, h, pt, sel: (b, 0, h)),
            scratch_shapes=[
                pltpu.VMEM((SAMPLE_SLOTS, rows, hd), F32),
                pltpu.VMEM((SAMPLE_SLOTS, rows, hd), F32),
                pltpu.SemaphoreType.DMA((SAMPLE_SLOTS, 2)),
            ],
        ),
        out_shape=jax.ShapeDtypeStruct((bsz, s_len, n_heads * hd), F32),
        compiler_params=_params(("arbitrary", "arbitrary"), 2 * SAMPLE_SLOTS * rows * hd * 4 + 8 * rows * 8 * 4),
        name="moba_sample",
    )(page_table, sel_flat, zs3, zs3, zs3, zs3, cache_k, cache_v)


def _conv_sample_kernel(cv_ref, cg_ref, st_ref, w_ref, b_ref, lg_ref, lb_ref, u_ref, nst_ref, xp, ybuf):
    s_len = cv_ref.shape[1]
    n_ctx = st_ref.shape[1]
    cw = w_ref.shape[0]
    xp[pl.ds(0, n_ctx), :] = st_ref[0]
    xp[pl.ds(n_ctx, s_len), :] = cv_ref[0] * jax.nn.sigmoid(cg_ref[0])
    w = w_ref[...]
    for t in range(s_len):
        ybuf[pl.ds(t, 1), :] = jnp.sum(w * xp[pl.ds(t, cw), :], axis=0, keepdims=True)
    y = ybuf[pl.ds(0, s_len), :] + b_ref[...]
    mu = jnp.mean(y, axis=-1, keepdims=True)
    yc = y - mu
    var = jnp.mean(yc * yc, axis=-1, keepdims=True)
    yn = yc * lax.rsqrt(var + LN_EPS) * lg_ref[...] + lb_ref[...]
    u_ref[0] = _silu(yn)
    nst_ref[0] = xp[pl.ds(s_len, n_ctx), :]


def _conv_sample(zs3, cv_col0, cg_col0, state, w_dw, b_dw, ln_g, ln_b):
    bsz, s_len, _ = zs3.shape
    n_ctx, c = state.shape[1:]
    cw = w_dw.shape[0]
    assert n_ctx == cw - 1
    rows = -(-(n_ctx + s_len) // 8) * 8
    return pl.pallas_call(
        _conv_sample_kernel,
        grid=(bsz,),
        in_specs=[
            pl.BlockSpec((1, s_len, c), lambda b: (b, 0, cv_col0 // c)),
            pl.BlockSpec((1, s_len, c), lambda b: (b, 0, cg_col0 // c)),
            pl.BlockSpec((1, n_ctx, c), lambda b: (b, 0, 0)),
            pl.BlockSpec((cw, c), lambda b: (0, 0)),
            pl.BlockSpec((1, c), lambda b: (0, 0)),
            pl.BlockSpec((1, c), lambda b: (0, 0)),
            pl.BlockSpec((1, c), lambda b: (0, 0)),
        ],
        out_specs=[pl.BlockSpec((1, s_len, c), lambda b: (b, 0, 0)), pl.BlockSpec((1, n_ctx, c), lambda b: (b, 0, 0))],
        out_shape=[jax.ShapeDtypeStruct((bsz, s_len, c), F32), jax.ShapeDtypeStruct((bsz, n_ctx, c), F32)],
        scratch_shapes=[pltpu.VMEM((rows, c), F32), pltpu.VMEM((8, c), F32)],
        compiler_params=_params(("parallel",), 16 * rows * c * 4),
        name="conv_sample",
    )(zs3, zs3, state, w_dw, b_dw.reshape(1, c), ln_g.reshape(1, c), ln_b.reshape(1, c))


def kernel(x_prompt, x_sample, cache_k, cache_v, state_conv, page_table, norm_in_g, w_in, w_dw, b_dw, ln_g, ln_b,
           w_pw2, w_out, norm_f_g):
    bp, t, d = x_prompt.shape
    bs, s_len, _ = x_sample.shape
    depth = w_in.shape[0]
    n_heads, hd = cache_k.shape[3:]
    att_w = n_heads * hd
    c = state_conv.shape[-1]
    n_ctx = state_conv.shape[2]
    assert att_w == c, "column-block addressing of the combined projection assumes equal group widths"
    q0, k0, v0, ga0 = 0, att_w, 2 * att_w, 3 * att_w
    cv0, cg0, gc0 = 4 * att_w, 4 * att_w + c, 4 * att_w + 2 * c

    xp = x_prompt.reshape(bp * t, d)
    xs = x_sample.reshape(bs * s_len, d)
    outs = {name: [] for name in ("kp", "vp", "cp", "ks", "vs", "cs")}
    for l in range(depth):
        w_in_b = w_in[l].astype(BF16)

        hp = _rms_norm(xp, norm_in_g[l], BF16)
        zp, kmean_s = _matmul_kmean(hp, w_in_b, cache_k, l, page_table)
        z_att, z_conv, kp, vp, cp = [], [], [], [], []
        for bi in range(bp):
            zb = zp[bi * t:(bi + 1) * t] if bp > 1 else zp
            sel_p = _gate_prompt(zb, n_heads, hd, q0, k0)
            za, k_rows, v_rows = _moba_prompt(zb, sel_p, n_heads, hd, q0, k0, v0, ga0)
            z_att.append(za)
            u, tail = _conv_prompt(zb, cv0, cg0, c, jnp.zeros((n_ctx, c), F32), w_dw[l], b_dw[l], ln_g[l], ln_b[l])
            z_conv.append(_matmul(u, w_pw2[l], 0, c, BF16, gate=zb, gate_col0=gc0))
            kp.append(k_rows)
            vp.append(v_rows)
            cp.append(tail)
        z_att = z_att[0] if bp == 1 else jnp.concatenate(z_att, axis=0)
        z_conv = z_conv[0] if bp == 1 else jnp.concatenate(z_conv, axis=0)
        xp = _outproj(z_att, z_conv, w_out[l], xp)
        outs["kp"].append(jnp.stack(kp)); outs["vp"].append(jnp.stack(vp)); outs["cp"].append(jnp.stack(cp))

        hs = _rms_norm(xs, norm_in_g[l], BF16)
        zs = _matmul(hs, w_in_b, 0, w_in_b.shape[1], F32)
        zs3 = zs.reshape(bs, s_len, zs.shape[1])
        sel = _gate_sample(zs3, kmean_s, n_heads, hd, q0)[:, :, :MOBA_TOP_K].reshape(-1)
        z_att_s = _moba_sample(zs3, sel, page_table, cache_k, cache_v, l, n_heads, hd, q0, k0, v0, ga0)
        u_s, new_state = _conv_sample(zs3, cv0, cg0, state_conv[l], w_dw[l], b_dw[l], ln_g[l], ln_b[l])
        z_conv_s = _matmul(u_s.reshape(bs * s_len, c), w_pw2[l], 0, c, BF16, gate=zs, gate_col0=gc0)
        xs = _outproj(z_att_s.reshape(bs * s_len, att_w), z_conv_s, w_out[l], xs)
        outs["ks"].append(zs[:, k0:k0 + att_w].reshape(bs, s_len, n_heads, hd))
        outs["vs"].append(zs[:, v0:v0 + att_w].reshape(bs, s_len, n_heads, hd))
        outs["cs"].append(new_state)

    y_prompt = _rms_norm(xp, norm_f_g, F32).reshape(bp, t, d)
    y_sample = _rms_norm(xs, norm_f_g, F32).reshape(bs, s_len, d)
    return (y_prompt, y_sample, jnp.stack(outs["kp"]), jnp.stack(outs["vp"]), jnp.stack(outs["cp"]),
            jnp.stack(outs["ks"]), jnp.stack(outs["vs"]), jnp.stack(outs["cs"]))
```

```python
import functools
import math

import jax
import jax.numpy as jnp
from jax import lax
from jax.experimental import pallas as pl
from jax.experimental.pallas import tpu as pltpu

MOBA_BLOCK = 256
MOBA_TOP_K = 3
RMS_EPS = 1e-6
LN_EPS = 1e-5
LOG2E = math.log2(math.e)
NEG_BIG = -1e30
POS_BIG = 1e30
CONV_HALO = 32
SUBLANES = 8
CONV_ROWS = 32
CONV_LANES = 256
ONES_ROWS = 16
SAMPLE_SLOTS = 3
PROMPT_UNROLL = 8
PROMPT_QBLOCKS = 4
LANES = 128
MM_ROWS = 1024
MM_COLS = 512
ROW_TILE = 256
GATE_QUERIES = 2048
MIN_SCOPED_VMEM_BYTES = 16 * 1024 * 1024
V7X_SCOPED_VMEM_BYTES = 60000 * 1024
F32 = jnp.float32
BF16 = jnp.bfloat16


def _params(semantics, vmem_bytes):
    limit = int(min(V7X_SCOPED_VMEM_BYTES, max(vmem_bytes, MIN_SCOPED_VMEM_BYTES)))
    return pltpu.CompilerParams(dimension_semantics=semantics, vmem_limit_bytes=limit)


def _tile(n, pref, unit):
    if n <= pref:
        return n
    t = (pref // unit) * unit
    while t >= unit:
        if n % t == 0:
            return t
        t -= unit
    raise ValueError(f"no tile for {n} (pref {pref}, unit {unit})")


def _silu(x):
    return x * jax.nn.sigmoid(x)


def _rms_kernel(x_ref, g_ref, o_ref):
    x = x_ref[...]
    ms = jnp.mean(x * x, axis=-1, keepdims=True)
    y = x * lax.rsqrt(ms + RMS_EPS)
    o_ref[...] = (y * g_ref[...]).astype(o_ref.dtype)


def _rms_norm(x, g, out_dtype):
    m, d = x.shape
    tm = _tile(m, ROW_TILE, SUBLANES)
    blk = tm * d * 4
    return pl.pallas_call(
        _rms_kernel,
        grid=(m // tm,),
        in_specs=[pl.BlockSpec((tm, d), lambda i: (i, 0)), pl.BlockSpec((1, d), lambda i: (0, 0))],
        out_specs=pl.BlockSpec((tm, d), lambda i: (i, 0)),
        out_shape=jax.ShapeDtypeStruct((m, d), out_dtype),
        compiler_params=_params(("parallel",), 6 * blk),
        name="rms_norm",
    )(x, g.reshape(1, d))


def _mm_kernel(*refs, gated):
    if gated:
        a_ref, b_ref, g_ref, o_ref = refs
    else:
        a_ref, b_ref, o_ref = refs
    acc = jnp.dot(a_ref[...].astype(BF16), b_ref[...].astype(BF16), preferred_element_type=F32)
    if gated:
        acc = acc * _silu(g_ref[...])
    o_ref[...] = acc.astype(o_ref.dtype)


def _matmul(a, b, col0, n, out_dtype, gate=None, gate_col0=0):
    m, k = a.shape
    tm = _tile(m, MM_ROWS, SUBLANES)
    tn = _tile(n, MM_COLS, LANES)
    assert col0 % tn == 0 and gate_col0 % tn == 0
    cb, gb = col0 // tn, gate_col0 // tn
    in_specs = [pl.BlockSpec((tm, k), lambda i, j: (i, 0)), pl.BlockSpec((k, tn), lambda i, j: (0, j + cb))]
    args = [a, b]
    if gate is not None:
        in_specs.append(pl.BlockSpec((tm, tn), lambda i, j: (i, j + gb)))
        args.append(gate)
    vmem = 2 * (tm * k * a.dtype.itemsize + k * tn * b.dtype.itemsize + 3 * tm * tn * 4) + (tm + tn) * k * 2
    return pl.pallas_call(
        functools.partial(_mm_kernel, gated=gate is not None),
        grid=(m // tm, n // tn),
        in_specs=in_specs,
        out_specs=pl.BlockSpec((tm, tn), lambda i, j: (i, j)),
        out_shape=jax.ShapeDtypeStruct((m, n), out_dtype),
        compiler_params=_params(("parallel", "parallel"), vmem),
        name="matmul",
    )(*args)


def _mm_kmean_kernel(pt_ref, a_ref, b_ref, ck_ref, o_ref, km_ref, pbuf, sem, *,
                     layer, n_blocks, seq_blocks, blocks_per_step, n_j):
    step = pl.program_id(0) * n_j + pl.program_id(1)
    ppb = pbuf.shape[1] // blocks_per_step
    n_heads, hd = pbuf.shape[-2:]
    n_active = -(-n_blocks // blocks_per_step)

    def copies(st, slot):
        out = []
        for r in range(blocks_per_step):
            g = jnp.minimum(st * blocks_per_step + r, n_blocks - 1)
            for p in range(ppb):
                pg = pt_ref[g // seq_blocks, (g % seq_blocks) * ppb + p]
                out.append(pltpu.make_async_copy(ck_ref.at[layer, pg], pbuf.at[slot, r * ppb + p], sem.at[slot]))
        return out

    @pl.when(step == 0)
    def _():
        for cp in copies(0, 0):
            cp.start()

    @pl.when(step + 1 < n_active)
    def _():
        for cp in copies(step + 1, (step + 1) % 2):
            cp.start()

    slot = step % 2

    @pl.when(step < n_active)
    def _():
        for cp in copies(step, slot):
            cp.wait()

    means = []
    for r in range(blocks_per_step):
        tot = jnp.sum(pbuf[slot, r * ppb], axis=0)
        for p in range(1, ppb):
            tot = tot + jnp.sum(pbuf[slot, r * ppb + p], axis=0)
        means.append(tot * (1.0 / MOBA_BLOCK))
    o_ref[...] = jnp.dot(a_ref[...], b_ref[...], preferred_element_type=F32)

    @pl.when(step < n_active)
    def _():
        for r in range(blocks_per_step):
            for h in range(n_heads):
                km_ref[r, :, pl.ds(h * hd, hd)] = means[r][h:h + 1, :]


def _matmul_kmean(a, b, cache_k, layer, page_table):
    m, k = a.shape
    n = b.shape[1]
    _, _, page, n_heads, hd = cache_k.shape
    bsz, n_pages = page_table.shape
    ppb = MOBA_BLOCK // page
    seq_blocks = n_pages // ppb
    n_blocks = bsz * seq_blocks
    tm = _tile(m, MM_ROWS, SUBLANES)
    tn = _tile(n, MM_COLS, LANES)
    n_i, n_j = m // tm, n // tn
    blocks_per_step = -(-n_blocks // (n_i * n_j))
    n_active = -(-n_blocks // blocks_per_step)
    pages = blocks_per_step * ppb
    vmem = 2 * (tm * k * 2 + k * tn * 2 + 2 * tm * tn * 4) + 2 * pages * page * n_heads * hd * 4
    z, km = pl.pallas_call(
        functools.partial(_mm_kmean_kernel, layer=layer, n_blocks=n_blocks, seq_blocks=seq_blocks,
                          blocks_per_step=blocks_per_step, n_j=n_j),
        grid_spec=pltpu.PrefetchScalarGridSpec(
            num_scalar_prefetch=1,
            grid=(n_i, n_j),
            in_specs=[
                pl.BlockSpec((tm, k), lambda i, j, pt: (i, 0)),
                pl.BlockSpec((k, tn), lambda i, j, pt: (0, j)),
                pl.BlockSpec(memory_space=pl.ANY),
            ],
            out_specs=[
                pl.BlockSpec((tm, tn), lambda i, j, pt: (i, j)),
                pl.BlockSpec((blocks_per_step, 1, n_heads * hd),
                             lambda i, j, pt: (jnp.minimum(i * n_j + j, n_active - 1), 0, 0)),
            ],
            scratch_shapes=[
                pltpu.VMEM((2, pages, page, n_heads, hd), F32),
                pltpu.SemaphoreType.DMA((2,)),
            ],
        ),
        out_shape=[
            jax.ShapeDtypeStruct((m, n), F32),
            jax.ShapeDtypeStruct((n_active * blocks_per_step, 1, n_heads * hd), F32),
        ],
        compiler_params=_params(("arbitrary", "arbitrary"), vmem),
        name="matmul_kmean",
    )(page_table, a, b, cache_k)
    return z, km[:n_blocks].reshape(bsz, seq_blocks, 1, n_heads * hd)


def _outproj_kernel(a1_ref, a2_ref, w1_ref, w2_ref, x_ref, o_ref):
    acc = jnp.dot(a1_ref[...].astype(BF16), w1_ref[...].astype(BF16), preferred_element_type=F32)
    acc += jnp.dot(a2_ref[...].astype(BF16), w2_ref[...].astype(BF16), preferred_element_type=F32)
    o_ref[...] = x_ref[...] + acc


def _outproj(a1, a2, w, x):
    m, k = a1.shape
    n = x.shape[1]
    assert a2.shape[1] == k and w.shape[0] == 2 * k
    tm = _tile(m, MM_ROWS, SUBLANES)
    tn = _tile(n, MM_COLS, LANES)
    vmem = (2 * (tm * k * (a1.dtype.itemsize + a2.dtype.itemsize) + 2 * k * tn * w.dtype.itemsize + 3 * tm * tn * 4)
            + 2 * (tm + tn) * k * 2)
    return pl.pallas_call(
        _outproj_kernel,
        grid=(m // tm, n // tn),
        in_specs=[
            pl.BlockSpec((tm, k), lambda i, j: (i, 0)),
            pl.BlockSpec((tm, k), lambda i, j: (i, 0)),
            pl.BlockSpec((k, tn), lambda i, j: (0, j)),
            pl.BlockSpec((k, tn), lambda i, j: (1, j)),
            pl.BlockSpec((tm, tn), lambda i, j: (i, j)),
        ],
        out_specs=pl.BlockSpec((tm, tn), lambda i, j: (i, j)),
        out_shape=jax.ShapeDtypeStruct((m, n), F32),
        compiler_params=_params(("parallel", "parallel"), vmem),
        name="outproj",
    )(a1, a2, w, w, x)


def _topk_first_indices(gate, blk, axis=-1):
    nb = gate.shape[axis]
    firsts = []
    for _ in range(MOBA_TOP_K):
        mx = jnp.max(gate, axis=axis, keepdims=True)
        first = jnp.min(jnp.where(gate == mx, blk, float(nb)), axis=axis, keepdims=True)
        firsts.append(first)
        gate = jnp.where(blk == first, -jnp.inf, gate)
    return firsts


def _gate_prompt_kernel(q_ref, k_ref, o_ref, km_sc):
    j = pl.program_id(1)
    tq = q_ref.shape[0]
    nb = km_sc.shape[0]

    @pl.when(j == 0)
    def _():
        for n in range(nb):
            km_sc[pl.ds(n, 1), :] = (jnp.sum(k_ref[pl.ds(n * MOBA_BLOCK, MOBA_BLOCK), :], axis=0, keepdims=True)
                                     * (1.0 / MOBA_BLOCK))

    gate = lax.dot_general(km_sc[...], q_ref[...], (((1,), (1,)), ((), ())),
                           precision=lax.Precision.HIGHEST, preferred_element_type=F32)
    blk = lax.broadcasted_iota(jnp.int32, (nb, tq), 0)
    own = (lax.broadcasted_iota(jnp.int32, (nb, tq), 1) + j * tq) // MOBA_BLOCK
    past = blk < own
    blk_f = blk.astype(F32)
    firsts = _topk_first_indices(jnp.where(past, gate, -jnp.inf), blk_f, axis=0)
    sel = jnp.zeros((nb, tq), F32)
    for first in firsts:
        sel = jnp.maximum(sel, jnp.where(blk_f == first, 1.0, 0.0))
    sel = jnp.where(past, sel, 0.0)
    o_ref[0] = jnp.where(blk == own, 1.0, sel)


def _gate_prompt(z, n_heads, hd, q_col0, k_col0):
    t = z.shape[0]
    assert t % MOBA_BLOCK == 0
    nb = t // MOBA_BLOCK
    tq = _tile(t, GATE_QUERIES, MOBA_BLOCK)
    return pl.pallas_call(
        _gate_prompt_kernel,
        grid=(n_heads, t // tq),
        in_specs=[
            pl.BlockSpec((tq, hd), lambda h, j: (j, q_col0 // hd + h)),
            pl.BlockSpec((t, hd), lambda h, j: (0, k_col0 // hd + h)),
        ],
        out_specs=pl.BlockSpec((1, nb, tq), lambda h, j: (h, 0, j)),
        out_shape=jax.ShapeDtypeStruct((n_heads, nb, t), F32),
        scratch_shapes=[pltpu.VMEM((nb, hd), F32)],
        compiler_params=_params(("parallel", "arbitrary"), 4 * (tq + t) * hd * 4 + 24 * nb * tq * 4),
        name="gate_prompt",
    )(z, z)


def _moba_prompt_kernel(q_ref, qn_ref, k_ref, v_ref, sel_ref, ga_ref, o_ref, ko_ref, vo_ref, kb_sc, vt_sc, bias_sc,
                        s_sc, bm_sc, acc_sc, m_sc, slot_sc, kv_sem, *, scale):
    i = pl.program_id(1)
    bq, hd = q_ref.shape
    qblocks = bq // MOBA_BLOCK
    n_chunk_total = vt_sc.shape[0]
    nt = (((1,), (1,)), ((), ()))
    chunk = PROMPT_UNROLL * MOBA_BLOCK

    head = pl.program_id(0)
    kv_out = (pltpu.make_async_copy(k_ref, ko_ref.at[:, head, :], kv_sem.at[0]),
              pltpu.make_async_copy(v_ref, vo_ref.at[:, head, :], kv_sem.at[1]))

    @pl.when(i == 0)
    def _():
        for cp in kv_out:
            cp.start()

    @pl.when(i == pl.num_programs(1) - 1)
    def _():
        for cp in kv_out:
            cp.wait()

    @pl.when(i == 0)
    def _():
        kb_sc[...] = k_ref[...].astype(BF16)
        key = lax.broadcasted_iota(jnp.int32, (MOBA_BLOCK, bq), 0)
        qry = lax.broadcasted_iota(jnp.int32, (MOBA_BLOCK, bq), 1)
        bias_sc[0] = jnp.zeros((MOBA_BLOCK, bq), F32)
        for r in range(qblocks):
            bias_sc[1 + r] = jnp.where(qry - r * MOBA_BLOCK >= key, 0.0, NEG_BIG)

        def stage(c, carry):
            for u in range(PROMPT_UNROLL):
                off = pl.multiple_of((c * PROMPT_UNROLL + u) * MOBA_BLOCK, MOBA_BLOCK)
                vt_sc[c, pl.ds(0, hd), pl.ds(u * MOBA_BLOCK, MOBA_BLOCK)] = (
                    v_ref[pl.ds(off, MOBA_BLOCK), :].T.astype(BF16))
            vt_sc[c, pl.ds(hd, ONES_ROWS), :] = jnp.ones((ONES_ROWS, chunk), BF16)
            return carry

        lax.fori_loop(0, n_chunk_total, stage, 0)

    qb = (q_ref[...] * (scale * LOG2E)).astype(BF16)
    qb_next = (qn_ref[...] * (scale * LOG2E)).astype(BF16)
    first_own = i * qblocks
    first_own_next = jnp.minimum(i + 1, pl.num_programs(1) - 1) * qblocks
    n_chunks = (first_own + qblocks - 1) // PROMPT_UNROLL + 1

    def scores_into(slot, q_bf16, tile_first_own, c):
        off = pl.multiple_of(c * chunk, chunk)
        s = lax.dot_general(kb_sc[pl.ds(off, chunk), :], q_bf16, nt, preferred_element_type=F32)
        for u in range(PROMPT_UNROLL):
            r = c * PROMPT_UNROLL + u - tile_first_own
            which = jnp.where((r >= 0) & (r < qblocks), r + 1, 0)
            su = s[u * MOBA_BLOCK:(u + 1) * MOBA_BLOCK] + bias_sc[which]
            s_sc[slot, pl.ds(u * MOBA_BLOCK, MOBA_BLOCK), :] = su
            bm_sc[slot, pl.ds(u, 1), :] = jnp.max(su, axis=0, keepdims=True)

    @pl.when(i == 0)
    def _():
        slot_sc[0] = 0
        scores_into(0, qb, first_own, 0)

    slot0 = slot_sc[0]
    acc_sc[...] = jnp.zeros(acc_sc.shape, F32)
    m_sc[...] = jnp.full(m_sc.shape, NEG_BIG, F32)

    def trip(c, carry):
        cur = (slot0 + c) % 2
        m_old = m_sc[...]
        picked = sel_ref[0, pl.ds(pl.multiple_of(c * PROMPT_UNROLL, PROMPT_UNROLL), PROMPT_UNROLL), :] > 0.5
        m_new = jnp.maximum(m_old, jnp.max(jnp.where(picked, bm_sc[cur], NEG_BIG), axis=0, keepdims=True))
        ps = []
        for u in range(PROMPT_UNROLL):
            m_eff = jnp.where(picked[u:u + 1], m_new, POS_BIG)
            ps.append(jnp.exp2(s_sc[cur, pl.ds(u * MOBA_BLOCK, MOBA_BLOCK), :] - m_eff).astype(BF16))
        pv = jnp.dot(vt_sc[c], jnp.concatenate(ps, axis=0), preferred_element_type=F32)
        acc_sc[...] = acc_sc[...] * jnp.exp2(m_old - m_new) + pv
        m_sc[...] = m_new
        last = c + 1 == n_chunks
        scores_into(1 - cur, jnp.where(last, qb_next, qb), jnp.where(last, first_own_next, first_own),
                    jnp.where(last, 0, c + 1))
        return carry

    lax.fori_loop(0, n_chunks, trip, 0)
    slot_sc[0] = (slot0 + n_chunks) % 2

    acc = acc_sc[...]
    out = (acc[:hd] / acc[hd:hd + 1]).T
    o_ref[...] = (out * _silu(ga_ref[...])).astype(o_ref.dtype)


def _moba_prompt(z, sel, n_heads, hd, q_col0, k_col0, v_col0, ga_col0):
    t = z.shape[0]
    chunk = PROMPT_UNROLL * MOBA_BLOCK
    bq = PROMPT_QBLOCKS * MOBA_BLOCK
    assert t % chunk == 0 and t % bq == 0
    nb = t // MOBA_BLOCK
    qc, kc, vc, gc = (c // hd for c in (q_col0, k_col0, v_col0, ga_col0))
    scratch = [
        pltpu.VMEM((t, hd), BF16),
        pltpu.VMEM((t // chunk, hd + ONES_ROWS, chunk), BF16),
        pltpu.VMEM((1 + PROMPT_QBLOCKS, MOBA_BLOCK, bq), F32),
        pltpu.VMEM((2, chunk, bq), F32),
        pltpu.VMEM((2, PROMPT_UNROLL, bq), F32),
        pltpu.VMEM((hd + ONES_ROWS, bq), F32),
        pltpu.VMEM((1, bq), F32),
        pltpu.SMEM((1,), jnp.int32),
        pltpu.SemaphoreType.DMA((2,)),
    ]
    n_tiles = t // bq
    vmem = (2 * (2 * t * hd * 4 + 4 * bq * hd * 4 + nb * bq * 4)
            + t * hd * 2 + (t // chunk) * (hd + ONES_ROWS) * chunk * 2 + (1 + PROMPT_QBLOCKS) * MOBA_BLOCK * bq * 4
            + 5 * chunk * bq * 4)
    return pl.pallas_call(
        functools.partial(_moba_prompt_kernel, scale=hd ** -0.5),
        grid=(n_heads, n_tiles),
        in_specs=[
            pl.BlockSpec((bq, hd), lambda h, i: (i, qc + h)),
            pl.BlockSpec((bq, hd), lambda h, i: (jnp.minimum(i + 1, n_tiles - 1), qc + h)),
            pl.BlockSpec((t, hd), lambda h, i: (0, kc + h)),
            pl.BlockSpec((t, hd), lambda h, i: (0, vc + h)),
            pl.BlockSpec((1, nb, bq), lambda h, i: (h, 0, i)),
            pl.BlockSpec((bq, hd), lambda h, i: (i, gc + h)),
        ],
        out_specs=[
            pl.BlockSpec((bq, hd), lambda h, i: (i, h)),
            pl.BlockSpec(memory_space=pl.ANY),
            pl.BlockSpec(memory_space=pl.ANY),
        ],
        out_shape=[
            jax.ShapeDtypeStruct((t, n_heads * hd), BF16),
            jax.ShapeDtypeStruct((t, n_heads, hd), F32),
            jax.ShapeDtypeStruct((t, n_heads, hd), F32),
        ],
        scratch_shapes=scratch,
        compiler_params=_params(("arbitrary", "arbitrary"), vmem),
        name="moba_prompt",
    )(z, z, z, z, sel, z)


def _conv_prompt_kernel(cv_ref, cvp_ref, cg_ref, cgp_ref, ctx_ref, w_ref, b_ref, lg_ref, lb_ref,
                        u_ref, tail_ref, gbuf, shifted, ybuf, wb):
    i = pl.program_id(0)
    tt = cv_ref.shape[0]
    cw = w_ref.shape[0]
    glu = cv_ref[...] * jax.nn.sigmoid(cg_ref[...])
    gbuf[pl.ds(CONV_HALO, tt), :] = glu

    @pl.when(i == 0)
    def _():
        gbuf[pl.ds(0, CONV_HALO), :] = ctx_ref[...]

    @pl.when(i > 0)
    def _():
        gbuf[pl.ds(0, CONV_HALO), :] = cvp_ref[...] * jax.nn.sigmoid(cgp_ref[...])

    span = shifted.shape[1]
    for s in range(1, SUBLANES):
        shifted[s - 1] = gbuf[pl.ds(s, span), :]
    base = CONV_HALO - (cw - 1)

    @pl.when(i == 0)
    def _():
        for k in range(cw):
            wb[k] = jnp.broadcast_to(w_ref[k:k + 1, :], wb.shape[1:])

    def rows_chunk(j, carry):
        r0 = pl.multiple_of(j * CONV_ROWS, CONV_ROWS)
        for c0 in range(0, ybuf.shape[1], CONV_LANES):
            lanes = pl.ds(c0, CONV_LANES)
            acc = None
            for s in range(SUBLANES):
                wholes = [w for w in range((base + cw - 1) // SUBLANES + 1) if 0 <= w * SUBLANES + s - base < cw]
                rows = pl.ds(r0 + wholes[0] * SUBLANES, CONV_ROWS + (wholes[-1] - wholes[0]) * SUBLANES)
                xs = gbuf[rows, lanes] if s == 0 else shifted[s - 1, rows, lanes]
                for w in wholes:
                    k = w * SUBLANES + s - base
                    x = xs[(w - wholes[0]) * SUBLANES:(w - wholes[0]) * SUBLANES + CONV_ROWS]
                    wk = jnp.concatenate([wb[k, :, lanes]] * (CONV_ROWS // SUBLANES), axis=0)
                    acc = wk * x if acc is None else acc + wk * x
            ybuf[pl.ds(r0, CONV_ROWS), lanes] = acc
        return carry

    lax.fori_loop(0, tt // CONV_ROWS, rows_chunk, 0)
    y = ybuf[...] + b_ref[...]
    mu = jnp.mean(y, axis=-1, keepdims=True)
    yc = y - mu
    var = jnp.mean(yc * yc, axis=-1, keepdims=True)
    yn = yc * lax.rsqrt(var + LN_EPS) * lg_ref[...] + lb_ref[...]
    u_ref[...] = _silu(yn).astype(u_ref.dtype)
    tail_ref[...] = gbuf[pl.ds(tt, CONV_HALO), :]


def _conv_prompt(z, cv_col0, cg_col0, c, ctx, w_dw, b_dw, ln_g, ln_b):
    t = z.shape[0]
    cw = w_dw.shape[0]
    n_ctx = cw - 1
    assert n_ctx <= CONV_HALO
    tt = _tile(t, ROW_TILE, CONV_HALO)
    assert t % tt == 0 and tt % CONV_HALO == 0
    r = tt // CONV_HALO
    cvb, cgb = cv_col0 // c, cg_col0 // c
    ctx_pad = jnp.concatenate([jnp.zeros((CONV_HALO - n_ctx, c), F32), ctx], axis=0)
    prev = lambda i: jnp.maximum(i * r - 1, 0)
    assert tt % CONV_ROWS == 0 and CONV_ROWS % SUBLANES == 0 and CONV_HALO % SUBLANES == 0
    span = tt + CONV_HALO - SUBLANES
    vmem = (2 * (2 * tt * c * 4 + 3 * CONV_HALO * c * 4 + tt * c * 2 + (cw + 3) * c * 4)
            + ((tt + CONV_HALO) + (SUBLANES - 1) * span + 5 * tt) * c * 4)
    u, tail = pl.pallas_call(
        _conv_prompt_kernel,
        grid=(t // tt,),
        in_specs=[
            pl.BlockSpec((tt, c), lambda i: (i, cvb)),
            pl.BlockSpec((CONV_HALO, c), lambda i: (prev(i), cvb)),
            pl.BlockSpec((tt, c), lambda i: (i, cgb)),
            pl.BlockSpec((CONV_HALO, c), lambda i: (prev(i), cgb)),
            pl.BlockSpec((CONV_HALO, c), lambda i: (0, 0)),
            pl.BlockSpec((cw, c), lambda i: (0, 0)),
            pl.BlockSpec((1, c), lambda i: (0, 0)),
            pl.BlockSpec((1, c), lambda i: (0, 0)),
            pl.BlockSpec((1, c), lambda i: (0, 0)),
        ],
        out_specs=[pl.BlockSpec((tt, c), lambda i: (i, 0)), pl.BlockSpec((CONV_HALO, c), lambda i: (0, 0))],
        out_shape=[jax.ShapeDtypeStruct((t, c), BF16), jax.ShapeDtypeStruct((CONV_HALO, c), F32)],
        scratch_shapes=[
            pltpu.VMEM((CONV_HALO + tt, c), F32),
            pltpu.VMEM((SUBLANES - 1, span, c), F32),
            pltpu.VMEM((tt, c), F32),
            pltpu.VMEM((cw, SUBLANES, c), F32),
        ],
        compiler_params=_params(("arbitrary",), vmem),
        name="conv_prompt",
    )(z, z, z, z, ctx_pad, w_dw, b_dw.reshape(1, c), ln_g.reshape(1, c), ln_b.reshape(1, c))
    return u, tail[CONV_HALO - n_ctx:]


def _gate_sample_kernel(q_ref, km_ref, o_ref, *, n_heads, hd):
    s = q_ref.shape[1]
    km = km_ref[0, :, 0, :]
    nbp, w = km.shape
    rows = [jnp.broadcast_to(q_ref[0, t:t + 1, :], (n_heads, w)) for t in range(s)]
    qbd = jnp.concatenate(rows, axis=0)
    r = lax.broadcasted_iota(jnp.int32, qbd.shape, 0) % n_heads
    c = lax.broadcasted_iota(jnp.int32, qbd.shape, 1) // hd
    qbd = jnp.where(r == c, qbd, 0.0)
    gate = lax.dot_general(qbd, km, (((1,), (1,)), ((), ())),
                           precision=lax.Precision.HIGHEST, preferred_element_type=F32)
    blk = lax.broadcasted_iota(jnp.int32, gate.shape, 1).astype(F32)
    firsts = _topk_first_indices(gate, blk)
    lane = lax.broadcasted_iota(jnp.int32, o_ref.shape[1:], 1)
    out = jnp.zeros(o_ref.shape[1:], F32)
    for j, first in enumerate(firsts):
        out = jnp.where(lane == j, first, out)
    o_ref[0] = out.astype(jnp.int32)


def _gate_sample(zs3, kmean, n_heads, hd, q_col0):
    bsz, s, _ = zs3.shape
    nbp = kmean.shape[1]
    w = n_heads * hd
    assert nbp >= MOBA_TOP_K and n_heads % 8 == 0 and q_col0 % w == 0
    return pl.pallas_call(
        functools.partial(_gate_sample_kernel, n_heads=n_heads, hd=hd),
        grid=(bsz,),
        in_specs=[
            pl.BlockSpec((1, s, w), lambda b: (b, 0, q_col0 // w)),
            pl.BlockSpec((1, nbp, 1, w), lambda b: (b, 0, 0, 0)),
        ],
        out_specs=pl.BlockSpec((1, s * n_heads, LANES), lambda b: (b, 0, 0)),
        out_shape=jax.ShapeDtypeStruct((bsz, s * n_heads, LANES), jnp.int32),
        compiler_params=_params(("parallel",), 8 * (s * n_heads + nbp * 8) * w * 4),
        name="gate_sample",
    )(zs3, kmean)


def _moba_sample_kernel(pt_ref, sel_ref, q_ref, kn_ref, vn_ref, ga_ref, ck_ref, cv_ref, o_ref,
                        kbuf, vbuf, sem, *, layer, scale, n_heads):
    b = pl.program_id(0)
    h = pl.program_id(1)
    n_b = pl.num_programs(0)
    s_len, hd = q_ref.shape[1:]
    page = ck_ref.shape[2]
    ppb = MOBA_BLOCK // page
    step = b * n_heads + h
    n_steps = n_b * n_heads

    def copies(bb, hh, slot):
        out = []
        for t in range(s_len):
            for j in range(MOBA_TOP_K):
                blk = sel_ref[((bb * s_len + t) * n_heads + hh) * MOBA_TOP_K + j]
                for r in range(ppb):
                    pg = pt_ref[bb, blk * ppb + r]
                    dst = pl.ds(((t * MOBA_TOP_K + j) * ppb + r) * page, page)
                    out.append(pltpu.make_async_copy(ck_ref.at[layer, pg, :, hh, :], kbuf.at[slot, dst, :],
                                                     sem.at[slot, 0]))
                    out.append(pltpu.make_async_copy(cv_ref.at[layer, pg, :, hh, :], vbuf.at[slot, dst, :],
                                                     sem.at[slot, 1]))
        return out

    def start_all(cps):
        for n, cp in enumerate(cps):
            cp.start(priority=n % 2)

    n_slots = kbuf.shape[0]
    ahead = n_slots - 1

    @pl.when(step == 0)
    def _():
        for st in range(ahead):
            if st < n_heads:
                start_all(copies(0, st, st))

    @pl.when(step + ahead < n_steps)
    def _():
        nxt = step + ahead
        start_all(copies(nxt // n_heads, nxt % n_heads, nxt % n_slots))

    slot = step % n_slots
    for cp in copies(b, h, slot):
        cp.wait()

    q = q_ref[0] * (scale * LOG2E)
    kk = kbuf[slot]
    vv = vbuf[slot]
    s = lax.dot_general(q, kk, (((1,), (1,)), ((), ())), preferred_element_type=F32)
    owner = lax.broadcasted_iota(jnp.int32, s.shape, 1) // (MOBA_TOP_K * MOBA_BLOCK)
    s = jnp.where(owner == lax.broadcasted_iota(jnp.int32, s.shape, 0), s, NEG_BIG)
    s_new = lax.dot_general(q, kn_ref[0], (((1,), (1,)), ((), ())), preferred_element_type=F32)
    causal = lax.broadcasted_iota(jnp.int32, s_new.shape, 1) <= lax.broadcasted_iota(jnp.int32, s_new.shape, 0)
    s_new = jnp.where(causal, s_new, NEG_BIG)
    m = jnp.maximum(jnp.max(s, axis=-1, keepdims=True), jnp.max(s_new, axis=-1, keepdims=True))
    p = jnp.exp2(s - m)
    p_new = jnp.exp2(s_new - m)
    l = jnp.sum(p, axis=-1, keepdims=True) + jnp.sum(p_new, axis=-1, keepdims=True)
    acc = jnp.dot(p, vv, preferred_element_type=F32) + jnp.dot(p_new, vn_ref[0], preferred_element_type=F32)
    o_ref[0] = (acc / l) * _silu(ga_ref[0])


def _moba_sample(zs3, sel_flat, page_table, cache_k, cache_v, layer, n_heads, hd, q_col0, k_col0, v_col0, ga_col0):
    bsz, s_len, _ = zs3.shape
    page = cache_k.shape[2]
    assert MOBA_BLOCK % page == 0 and (page_table.shape[1] * page) % MOBA_BLOCK == 0 and s_len <= MOBA_BLOCK
    assert n_heads >= SAMPLE_SLOTS - 1
    rows = s_len * MOBA_TOP_K * MOBA_BLOCK
    qc, kc, vc, gc = (c // hd for c in (q_col0, k_col0, v_col0, ga_col0))
    tok = lambda c0: pl.BlockSpec((1, s_len, hd), lambda b, h, pt, sel: (b, 0, c0 + h))
    return pl.pallas_call(
        functools.partial(_moba_sample_kernel, layer=layer, scale=hd ** -0.5, n_heads=n_heads),
        grid_spec=pltpu.PrefetchScalarGridSpec(
            num_scalar_prefetch=2,
            grid=(bsz, n_heads),
            in_specs=[tok(qc), tok(kc), tok(vc), tok(gc),
                      pl.BlockSpec(memory_space=pl.ANY), pl.BlockSpec(memory_space=pl.ANY)],
            out_specs=pl.BlockSpec((1, s_len, hd), lambda b, h, pt, sel: (b, 0, h)),
            scratch_shapes=[
                pltpu.VMEM((SAMPLE_SLOTS, rows, hd), F32),
                pltpu.VMEM((SAMPLE_SLOTS, rows, hd), F32),
                pltpu.SemaphoreType.DMA((SAMPLE_SLOTS, 2)),
            ],
        ),
        out_shape=jax.ShapeDtypeStruct((bsz, s_len, n_heads * hd), F32),
        compiler_params=_params(("arbitrary", "arbitrary"), 2 * SAMPLE_SLOTS * rows * hd * 4 + 8 * rows * 8 * 4),
        name="moba_sample",
    )(page_table, sel_flat, zs3, zs3, zs3, zs3, cache_k, cache_v)


def _conv_sample_kernel(cv_ref, cg_ref, st_ref, w_ref, b_ref, lg_ref, lb_ref, u_ref, nst_ref, xp, ybuf):
    s_len = cv_ref.shape[1]
    n_ctx = st_ref.shape[1]
    cw = w_ref.shape[0]
    xp[pl.ds(0, n_ctx), :] = st_ref[0]
    xp[pl.ds(n_ctx, s_len), :] = cv_ref[0] * jax.nn.sigmoid(cg_ref[0])
    w = w_ref[...]
    for t in range(s_len):
        ybuf[pl.ds(t, 1), :] = jnp.sum(w * xp[pl.ds(t, cw), :], axis=0, keepdims=True)
    y = ybuf[pl.ds(0, s_len), :] + b_ref[...]
    mu = jnp.mean(y, axis=-1, keepdims=True)
    yc = y - mu
    var = jnp.mean(yc * yc, axis=-1, keepdims=True)
    yn = yc * lax.rsqrt(var + LN_EPS) * lg_ref[...] + lb_ref[...]
    u_ref[0] = _silu(yn)
    nst_ref[0] = xp[pl.ds(s_len, n_ctx), :]


def _conv_sample(zs3, cv_col0, cg_col0, state, w_dw, b_dw, ln_g, ln_b):
    bsz, s_len, _ = zs3.shape
    n_ctx, c = state.shape[1:]
    cw = w_dw.shape[0]
    assert n_ctx == cw - 1
    rows = -(-(n_ctx + s_len) // 8) * 8
    return pl.pallas_call(
        _conv_sample_kernel,
        grid=(bsz,),
        in_specs=[
            pl.BlockSpec((1, s_len, c), lambda b: (b, 0, cv_col0 // c)),
            pl.BlockSpec((1, s_len, c), lambda b: (b, 0, cg_col0 // c)),
            pl.BlockSpec((1, n_ctx, c), lambda b: (b, 0, 0)),
            pl.BlockSpec((cw, c), lambda b: (0, 0)),
            pl.BlockSpec((1, c), lambda b: (0, 0)),
            pl.BlockSpec((1, c), lambda b: (0, 0)),
            pl.BlockSpec((1, c), lambda b: (0, 0)),
        ],
        out_specs=[pl.BlockSpec((1, s_len, c), lambda b: (b, 0, 0)), pl.BlockSpec((1, n_ctx, c), lambda b: (b, 0, 0))],
        out_shape=[jax.ShapeDtypeStruct((bsz, s_len, c), F32), jax.ShapeDtypeStruct((bsz, n_ctx, c), F32)],
        scratch_shapes=[pltpu.VMEM((rows, c), F32), pltpu.VMEM((8, c), F32)],
        compiler_params=_params(("parallel",), 16 * rows * c * 4),
        name="conv_sample",
    )(zs3, zs3, state, w_dw, b_dw.reshape(1, c), ln_g.reshape(1, c), ln_b.reshape(1, c))


def kernel(x_prompt, x_sample, cache_k, cache_v, state_conv, page_table, norm_in_g, w_in, w_dw, b_dw, ln_g, ln_b,
           w_pw2, w_out, norm_f_g):
    bp, t, d = x_prompt.shape
    bs, s_len, _ = x_sample.shape
    depth = w_in.shape[0]
    n_heads, hd = cache_k.shape[3:]
    att_w = n_heads * hd
    c = state_conv.shape[-1]
    n_ctx = state_conv.shape[2]
    assert att_w == c, "column-block addressing of the combined projection assumes equal group widths"
    q0, k0, v0, ga0 = 0, att_w, 2 * att_w, 3 * att_w
    cv0, cg0, gc0 = 4 * att_w, 4 * att_w + c, 4 * att_w + 2 * c

    xp = x_prompt.reshape(bp * t, d)
    xs = x_sample.reshape(bs * s_len, d)
    outs = {name: [] for name in ("kp", "vp", "cp", "ks", "vs", "cs")}
    for l in range(depth):
        w_in_b = w_in[l].astype(BF16)

        hp = _rms_norm(xp, norm_in_g[l], BF16)
        zp, kmean_s = _matmul_kmean(hp, w_in_b, cache_k, l, page_table)
        z_att, z_conv, kp, vp, cp = [], [], [], [], []
        for bi in range(bp):
            zb = zp[bi * t:(bi + 1) * t] if bp > 1 else zp
            sel_p = _gate_prompt(zb, n_heads, hd, q0, k0)
            za, k_rows, v_rows = _moba_prompt(zb, sel_p, n_heads, hd, q0, k0, v0, ga0)
            z_att.append(za)
            u, tail = _conv_prompt(zb, cv0, cg0, c, jnp.zeros((n_ctx, c), F32), w_dw[l], b_dw[l], ln_g[l], ln_b[l])
            z_conv.append(_matmul(u, w_pw2[l], 0, c, BF16, gate=zb, gate_col0=gc0))
            kp.append(k_rows)
            vp.append(v_rows)
            cp.append(tail)
        z_att = z_att[0] if bp == 1 else jnp.concatenate(z_att, axis=0)
        z_conv = z_conv[0] if bp == 1 else jnp.concatenate(z_conv, axis=0)
        xp = _outproj(z_att, z_conv, w_out[l], xp)
        outs["kp"].append(jnp.stack(kp)); outs["vp"].append(jnp.stack(vp)); outs["cp"].append(jnp.stack(cp))

        hs = _rms_norm(xs, norm_in_g[l], BF16)
        zs = _matmul(hs, w_in_b, 0, w_in_b.shape[1], F32)
        zs3 = zs.reshape(bs, s_len, zs.shape[1])
        sel = _gate_sample(zs3, kmean_s, n_heads, hd, q0)[:, :, :MOBA_TOP_K].reshape(-1)
        z_att_s = _moba_sample(zs3, sel, page_table, cache_k, cache_v, l, n_heads, hd, q0, k0, v0, ga0)
        u_s, new_state = _conv_sample(zs3, cv0, cg0, state_conv[l], w_dw[l], b_dw[l], ln_g[l], ln_b[l])
        z_conv_s = _matmul(u_s.reshape(bs * s_len, c), w_pw2[l], 0, c, BF16, gate=zs, gate_col0=gc0)
        xs = _outproj(z_att_s.reshape(bs * s_len, att_w), z_conv_s, w_out[l], xs)
        outs["ks"].append(zs[:, k0:k0 + att_w].reshape(bs, s_len, n_heads, hd))
        outs["vs"].append(zs[:, v0:v0 + att_w].reshape(bs, s_len, n_heads, hd))
        outs["cs"].append(new_state)

    y_prompt = _rms_norm(xp, norm_f_g, F32).reshape(bp, t, d)
    y_sample = _rms_norm(xs, norm_f_g, F32).reshape(bs, s_len, d)
    return (y_prompt, y_sample, jnp.stack(outs["kp"]), jnp.stack(outs["vp"]), jnp.stack(outs["cp"]),
            jnp.stack(outs["ks"]), jnp.stack(outs["vs"]), jnp.stack(outs["cs"]))
```
